```python
import jax, jax.numpy as jnp
from jax import lax
import numpy as np

D_MODEL = 1024
BATCH = 4
SEQ = 8192
DEPTH = 4

CTX_LEN = 256
GRID_W = 64
EPS = 1e-6
N_MOD = 6

MLSTM_HEADS = 4
MLSTM_HD = 128
MLSTM_W = MLSTM_HEADS * MLSTM_HD
MLSTM_CHUNK = 128
QK_CONV = 3

SGU_GROUPS = 4
SGU_CHUNK = 128
SGU_W = 512
SGU_GW = SGU_W // SGU_GROUPS

FNET_GROUPS = 4
FNET_W = 512
FNET_GW = FNET_W // FNET_GROUPS

N_BRANCH = 3
BRANCH_W = 512

PROJ_SPLITS = (MLSTM_W, MLSTM_W, MLSTM_W, MLSTM_W, 4 * MLSTM_HEADS, SGU_W, SGU_W, FNET_W, N_BRANCH * D_MODEL)
PROJ_W = 4 * MLSTM_W + 4 * MLSTM_HEADS + 2 * SGU_W + FNET_W + N_BRANCH * D_MODEL
GATE_OFF = 4 * MLSTM_W

PEER_HEADS = 8
N_KEYS = 128
N_EXPERTS = N_KEYS * N_KEYS
PEER_TOPK = 16
D_QUERY = D_MODEL // 4
D_SUB = D_QUERY // 2
PEER_BLOCK = 128

kernel_name = 'hybrid_mlstm_sgu_fnet_peer_prefix_dit'


def rmsnorm(x, g):
    xf = x.astype(jnp.float32)
    y = xf * lax.rsqrt(jnp.mean(xf * xf, axis=-1, keepdims=True) + EPS)
    return (y * g.astype(jnp.float32)).astype(x.dtype)


def modulate(h, shift, scale):
    return h * (1 + scale) + shift


def pos_embed_2d(rows, dtype):
    quarter = D_MODEL // 4
    omega = 1.0 / (10000.0 ** (jnp.arange(quarter, dtype=jnp.float32) / quarter))
    r = jnp.repeat(jnp.arange(rows, dtype=jnp.float32), GRID_W)[:, None] * omega
    cc = jnp.tile(jnp.arange(GRID_W, dtype=jnp.float32), rows)[:, None] * omega
    return jnp.concatenate([jnp.sin(r), jnp.cos(r), jnp.sin(cc), jnp.cos(cc)], axis=-1).astype(dtype)


def conv_centred(x, w):
    pad = w.shape[0] // 2
    return lax.conv_general_dilated(x, w[:, None, :].astype(x.dtype), window_strides=(1,), padding=[(pad, pad)], dimension_numbers=('NWC', 'WIO', 'NWC'), feature_group_count=x.shape[-1])


def mlstm_zero_state(batch):
    return (jnp.zeros((batch, MLSTM_HEADS, MLSTM_HD, MLSTM_HD), jnp.float32),
            jnp.zeros((batch, MLSTM_HEADS, MLSTM_HD), jnp.float32),
            jnp.zeros((batch, MLSTM_HEADS), jnp.float32))


def mlstm_scan(q, k, v, ig, lf, state):
    B, T, H, d = q.shape
    L = MLSTM_CHUNK
    nc = T // L
    to_chunks = lambda a: a.reshape(B, nc, L, H, d).transpose(1, 0, 3, 2, 4)
    gate_chunks = lambda a: a.reshape(B, nc, L, H).transpose(1, 0, 3, 2)
    earlier = jnp.tril(jnp.ones((L, L), dtype=bool))

    def step(carry, inp):
        C, n, m = carry
        qc, kc, vc, ic, fc = inp
        b = jnp.cumsum(fc, axis=-1)
        dmat = jnp.where(earlier, b[..., :, None] - b[..., None, :] + ic[..., None, :], -jnp.inf)
        inter = b + m[..., None]
        m_t = jnp.maximum(inter, jnp.max(dmat, axis=-1))
        s = jnp.einsum('bhtd,bhsd->bhts', qc, kc) * jnp.exp(dmat - m_t[..., None])
        a = jnp.exp(inter - m_t)
        num = jnp.einsum('bhts,bhse->bhte', s, vc) + a[..., None] * jnp.einsum('bhtd,bhde->bhte', qc, C)
        den = jnp.sum(s, axis=-1) + a * jnp.einsum('bhtd,bhd->bht', qc, n)
        h = num / jnp.maximum(jnp.abs(den), jnp.exp(-m_t))[..., None]
        b_last = b[..., -1]
        g = b_last[..., None] - b + ic
        m_new = jnp.maximum(b_last + m, jnp.max(g, axis=-1))
        w = jnp.exp(g - m_new[..., None])
        decay = jnp.exp(b_last + m - m_new)
        C_new = decay[..., None, None] * C + jnp.einsum('bhsd,bhse->bhde', kc * w[..., None], vc)
        n_new = decay[..., None] * n + jnp.einsum('bhs,bhsd->bhd', w, kc)
        return (C_new, n_new, m_new), h

    final, hs = lax.scan(step, state, (to_chunks(q), to_chunks(k), to_chunks(v), gate_chunks(ig), gate_chunks(lf)))
    return hs.transpose(1, 0, 3, 2, 4).reshape(B, T, H, d), final


def mlstm_branch(q_raw, k_raw, v_raw, o_raw, gate_raw, conv_w, norm_g, init_fwd, init_bwd):
    B, T, _ = q_raw.shape
    qk = jax.nn.silu(conv_centred(jnp.concatenate([q_raw, k_raw], axis=-1), conv_w))
    heads = lambda a: a.astype(jnp.float32).reshape(B, T, MLSTM_HEADS, MLSTM_HD)
    q = heads(qk[..., :MLSTM_W])
    k = heads(qk[..., MLSTM_W:]) * (MLSTM_HD ** -0.5)
    v = heads(v_raw)
    gates = gate_raw.astype(jnp.float32).reshape(B, T, 4, MLSTM_HEADS)
    i_f, lf_f = gates[:, :, 0], jax.nn.log_sigmoid(gates[:, :, 1])
    i_b, lf_b = gates[:, :, 2], jax.nn.log_sigmoid(gates[:, :, 3])
    h_f, st_f = mlstm_scan(q, k, v, i_f, lf_f, init_fwd)
    rev = lambda a: jnp.flip(a, axis=1)
    h_b, st_b = mlstm_scan(rev(q), rev(k), rev(v), rev(i_b), rev(lf_b), init_bwd)
    h = h_f + rev(h_b)
    mu = jnp.mean(h, axis=-1, keepdims=True)
    var = jnp.mean(jnp.square(h - mu), axis=-1, keepdims=True)
    h = (h - mu) * lax.rsqrt(var + EPS) * norm_g.astype(jnp.float32).reshape(MLSTM_HEADS, MLSTM_HD)
    out = jax.nn.sigmoid(o_raw.astype(jnp.float32)) * h.reshape(B, T, MLSTM_W)
    return out.astype(q_raw.dtype), st_f, st_b


def sgu_branch(u_raw, v_raw, w_s, b_s):
    B, T, _ = u_raw.shape
    u = jax.nn.gelu(u_raw)
    vf = jax.nn.gelu(v_raw).astype(jnp.float32)
    mu = jnp.mean(vf, axis=-1, keepdims=True)
    var = jnp.mean(jnp.square(vf - mu), axis=-1, keepdims=True)
    v = ((vf - mu) * lax.rsqrt(var + EPS)).astype(u_raw.dtype)
    v = v.reshape(B, T // SGU_CHUNK, SGU_CHUNK, SGU_GROUPS, SGU_GW)
    s = jnp.einsum('gts,bnsgc->bntgc', w_s, v) + b_s.T[None, None, :, :, None]
    return u * s.reshape(B, T, SGU_W)


def fnet_branch(z):
    B, T, _ = z.shape
    zf = z.astype(jnp.float32).reshape(B, T, FNET_GROUPS, FNET_GW)
    y = jnp.fft.fft2(zf, axes=(1, 3), norm='ortho').real
    return y.reshape(B, T, FNET_W).astype(z.dtype)


def token_mixers(h, w_in, b_in, conv_w, mnorm_g, sgu_w, sgu_b, w_br, w_out, init_fwd, init_bwd, with_output):
    B, T, _ = h.shape
    split_at = np.cumsum(PROJ_SPLITS)[:-1].tolist()
    q, k, v, o, g, su, sv, fz, mg = jnp.split(h @ w_in + b_in, split_at, axis=-1)
    y_m, st_f, st_b = mlstm_branch(q, k, v, o, g, conv_w, mnorm_g, init_fwd, init_bwd)
    if not with_output:
        return None, st_f, st_b
    branches = (y_m, sgu_branch(su, sv, sgu_w, sgu_b), fnet_branch(fz))
    gate = jax.nn.sigmoid(mg).reshape(B, T, N_BRANCH, D_MODEL)
    y = gate[:, :, 0] * (branches[0] @ w_br[0])
    for r in range(1, N_BRANCH):
        y = y + gate[:, :, r] * (branches[r] @ w_br[r])
    return y @ w_out, st_f, st_b


def peer_ffn(h, wq, keys, u_tab, v_tab):
    B, T, D = h.shape
    tokens = h.reshape(-1, PEER_BLOCK, D)

    def block(xb):
        nb = xb.shape[0]
        q = (xb @ wq).reshape(nb, PEER_HEADS, 2, D_SUB)
        s = jnp.einsum('thpc,pkc->thpk', q, keys)
        s_top, i_top = lax.top_k(s, PEER_TOPK)
        n_cand = PEER_TOPK * PEER_TOPK
        cand_s = (s_top[:, :, 0, :, None] + s_top[:, :, 1, None, :]).reshape(nb, PEER_HEADS, n_cand)
        cand_i = (i_top[:, :, 0, :, None] * N_KEYS + i_top[:, :, 1, None, :]).reshape(nb, PEER_HEADS, n_cand)
        best_s, best_j = lax.top_k(cand_s, PEER_TOPK)
        idx = jnp.take_along_axis(cand_i, best_j, axis=-1)
        w = jax.nn.softmax(best_s.astype(jnp.float32), axis=-1)
        u = jnp.take(u_tab, idx, axis=0)
        act = jax.nn.gelu(jnp.einsum('thkd,td->thk', u, xb).astype(jnp.float32)) * w
        v = jnp.take(v_tab, idx, axis=0)
        return jnp.einsum('thk,thkd->td', act.astype(xb.dtype), v)

    return lax.map(block, tokens).reshape(B, T, D)


def setup_inputs(seed: int = 0) -> dict:
    key = jax.random.key(seed)
    ks = jax.random.split(key, 24)
    nrm = lambda k, shape, s: jax.random.normal(k, shape, jnp.float32) * s
    D = D_MODEL
    x = nrm(ks[0], (BATCH, SEQ, D), 1.0)
    c = nrm(ks[1], (BATCH, D), 1.0)
    ctx = nrm(ks[2], (BATCH, CTX_LEN, D), 1.0)
    c_ctx = nrm(ks[3], (D,), 1.0)
    w_mod = nrm(ks[4], (DEPTH, D, N_MOD * D), 0.5 * D ** -0.5)
    b_mod = nrm(ks[5], (DEPTH, N_MOD * D), 0.02)
    norm1_g = 1.0 + nrm(ks[6], (DEPTH, D), 0.05)
    norm2_g = 1.0 + nrm(ks[7], (DEPTH, D), 0.05)
    w_in = nrm(ks[8], (DEPTH, D, PROJ_W), D ** -0.5)
    fbias = jnp.linspace(3.0, 6.0, MLSTM_HEADS, dtype=jnp.float32)
    b_in = nrm(ks[9], (DEPTH, PROJ_W), 0.02)
    b_in = b_in.at[:, GATE_OFF + MLSTM_HEADS:GATE_OFF + 2 * MLSTM_HEADS].add(fbias)
    b_in = b_in.at[:, GATE_OFF + 3 * MLSTM_HEADS:GATE_OFF + 4 * MLSTM_HEADS].add(fbias)
    conv_qk = nrm(ks[10], (DEPTH, QK_CONV, 2 * MLSTM_W), QK_CONV ** -0.5)
    mlstm_norm_g = 1.0 + nrm(ks[11], (DEPTH, MLSTM_W), 0.05)
    sgu_w = nrm(ks[12], (DEPTH, SGU_GROUPS, SGU_CHUNK, SGU_CHUNK), SGU_CHUNK ** -0.5)
    sgu_b = 1.0 + nrm(ks[13], (DEPTH, SGU_GROUPS, SGU_CHUNK), 0.05)
    w_br = nrm(ks[14], (DEPTH, N_BRANCH, BRANCH_W, D), BRANCH_W ** -0.5)
    w_out = nrm(ks[15], (DEPTH, D, D), D ** -0.5)
    peer_wq = nrm(ks[16], (DEPTH, D, PEER_HEADS * D_QUERY), D ** -0.5)
    peer_keys = nrm(ks[17], (DEPTH, 2, N_KEYS, D_SUB), D_SUB ** -0.5)
    peer_u = nrm(ks[18], (DEPTH, N_EXPERTS, D), D ** -0.5)
    peer_v = nrm(ks[19], (DEPTH, N_EXPERTS, D), PEER_HEADS ** -0.5)
    final_g = 1.0 + nrm(ks[20], (D,), 0.05)
    return {'x': x, 'c': c, 'ctx': ctx, 'c_ctx': c_ctx, 'w_mod': w_mod, 'b_mod': b_mod,
            'norm1_g': norm1_g, 'norm2_g': norm2_g, 'w_in': w_in, 'b_in': b_in, 'conv_qk': conv_qk,
            'mlstm_norm_g': mlstm_norm_g, 'sgu_w': sgu_w, 'sgu_b': sgu_b, 'w_br': w_br, 'w_out': w_out,
            'peer_wq': peer_wq, 'peer_keys': peer_keys, 'peer_u': peer_u, 'peer_v': peer_v, 'final_g': final_g}


def reference(x, c, ctx, c_ctx, w_mod, b_mod, norm1_g, norm2_g, w_in, b_in, conv_qk, mlstm_norm_g, sgu_w, sgu_b, w_br, w_out, peer_wq, peer_keys, peer_u, peer_v, final_g):
    B, T, D = x.shape
    rows = T // GRID_W
    x = x + pos_embed_2d(rows, x.dtype)[None]
    silu_c = jax.nn.silu(c)
    silu_cc = jax.nn.silu(c_ctx)
    zero = mlstm_zero_state(B)
    for l in range(DEPTH):
        last = l == DEPTH - 1
        mod_x = jnp.split((silu_c @ w_mod[l] + b_mod[l])[:, None, :], N_MOD, axis=-1)
        mod_c = jnp.split(silu_cc @ w_mod[l] + b_mod[l], N_MOD, axis=-1)
        mixer_w = (w_in[l], b_in[l], conv_qk[l], mlstm_norm_g[l], sgu_w[l], sgu_b[l], w_br[l], w_out[l])
        peer_w = (peer_wq[l], peer_keys[l], peer_u[l], peer_v[l])
        hc = modulate(rmsnorm(ctx, norm1_g[l]), mod_c[0], mod_c[1])
        yc, ctx_fwd, ctx_bwd = token_mixers(hc, *mixer_w, zero, zero, not last)
        hx = modulate(rmsnorm(x, norm1_g[l]), mod_x[0], mod_x[1])
        yx, _, _ = token_mixers(hx, *mixer_w, ctx_fwd, ctx_bwd, True)
        x = x + mod_x[2] * yx
        x = x + mod_x[5] * peer_ffn(modulate(rmsnorm(x, norm2_g[l]), mod_x[3], mod_x[4]), *peer_w)
        if not last:
            ctx = ctx + mod_c[2] * yc
            ctx = ctx + mod_c[5] * peer_ffn(modulate(rmsnorm(ctx, norm2_g[l]), mod_c[3], mod_c[4]), *peer_w)
    return rmsnorm(x, final_g)
```

```python
import functools

import numpy as np
import jax
import jax.numpy as jnp
from jax import lax
from jax.experimental import pallas as pl
from jax.experimental.pallas import tpu as pltpu

F32 = jnp.float32
BF16 = jnp.bfloat16
TABLE_DT = jnp.bfloat16
HIGHEST = lax.Precision.HIGHEST

D_MODEL = 1024
GRID_W = 64
EPS = 1e-6
N_MOD = 6
N_HEADS = 4
HEAD_D = 128
CHUNK = 128
MIX_W = 512
N_GROUPS = 4
GROUP_W = 128
PEER_HEADS = 8
N_KEYS = 128
TOPK = 16
N_SEL = PEER_HEADS * TOPK
HALF_D = D_MODEL // 2
SLAB = 4
VMEM_LIMIT = 56 * 1024 * 1024


def _params(*sem):
    return pltpu.CompilerParams(dimension_semantics=sem, vmem_limit_bytes=VMEM_LIMIT)


def _dot(a, b):
    return jnp.dot(a, b, preferred_element_type=F32)


def _dot_nt(a, b):
    return lax.dot_general(a, b, (((1,), (1,)), ((), ())), preferred_element_type=F32)


def _dot_tn(a, b):
    return lax.dot_general(a, b, (((0,), (0,)), ((), ())), preferred_element_type=F32)


def _dot_exact(a, b):
    return jnp.dot(a, b, preferred_element_type=F32, precision=HIGHEST)


def _norm_mod(x, g, shift, scale):
    y = x * lax.rsqrt(jnp.mean(x * x, axis=-1, keepdims=True) + EPS)
    return (y * g) * (1.0 + scale) + shift


def _add_pos_kernel(x_ref, p_ref, o_ref):
    o_ref[0] = x_ref[0] + p_ref[...]


def _add_pos(x, pos):
    B, T, D = x.shape
    tm = 512
    return pl.pallas_call(
        _add_pos_kernel,
        grid=(B, T // tm),
        in_specs=[pl.BlockSpec((1, tm, D), lambda b, i: (b, i, 0)),
                  pl.BlockSpec((tm, D), lambda b, i: (i, 0))],
        out_specs=pl.BlockSpec((1, tm, D), lambda b, i: (b, i, 0)),
        out_shape=jax.ShapeDtypeStruct(x.shape, x.dtype),
        compiler_params=_params("parallel", "parallel"),
        name="add_pos",
    )(x, pos)


def _final_norm_kernel(x_ref, g_ref, o_ref):
    x = x_ref[0]
    o_ref[0] = x * lax.rsqrt(jnp.mean(x * x, axis=-1, keepdims=True) + EPS) * g_ref[...]


def _final_norm(x, g):
    B, T, D = x.shape
    tm = 512
    return pl.pallas_call(
        _final_norm_kernel,
        grid=(B, T // tm),
        in_specs=[pl.BlockSpec((1, tm, D), lambda b, i: (b, i, 0)),
                  pl.BlockSpec((1, D), lambda b, i: (0, 0))],
        out_specs=pl.BlockSpec((1, tm, D), lambda b, i: (b, i, 0)),
        out_shape=jax.ShapeDtypeStruct(x.shape, x.dtype),
        compiler_params=_params("parallel", "parallel"),
        name="final_norm",
    )(x, g.reshape(1, D))


def _mod_kernel(c_ref, w_ref, b_ref, o_ref):
    c = c_ref[...]
    s = c * jax.nn.sigmoid(c)
    o_ref[0] = _dot_exact(s, w_ref[0]) + b_ref[0]


def _modulation(cc, w_mod, b_mod):
    depth, D, W = w_mod.shape
    return pl.pallas_call(
        _mod_kernel,
        grid=(depth, W // D),
        in_specs=[pl.BlockSpec((8, D), lambda l, j: (0, 0)),
                  pl.BlockSpec((1, D, D), lambda l, j: (l, 0, j)),
                  pl.BlockSpec((1, 1, D), lambda l, j: (l, 0, j))],
        out_specs=pl.BlockSpec((1, 8, D), lambda l, j: (l, 0, j)),
        out_shape=jax.ShapeDtypeStruct((depth, 8, W), F32),
        compiler_params=_params("parallel", "parallel"),
        name="adaln_mod",
    )(cc, w_mod, b_mod.reshape(depth, 1, W))


def _mproj_kernel(x_ref, sh_ref, sc_ref, g_ref, w_ref, b_ref, wg_ref, bg_ref, o_ref, og_ref):
    h = _norm_mod(x_ref[0], g_ref[...], sh_ref[0], sc_ref[0])
    o_ref[0] = _dot(h.astype(BF16), w_ref[...]) + b_ref[...]
    og_ref[0] = _dot_exact(h, wg_ref[...]) + bg_ref[...]


def _mlstm_proj(x, shift, scale, g, w, b, wg, bg):
    B, T, D = x.shape
    tm = min(512, T)
    W = w.shape[1]
    NG = wg.shape[1]
    tok = lambda b_, i: (b_, i, 0)
    per_b = lambda b_, i: (b_, 0, 0)
    const = lambda b_, i: (0, 0)
    return pl.pallas_call(
        _mproj_kernel,
        grid=(B, T // tm),
        in_specs=[pl.BlockSpec((1, tm, D), tok),
                  pl.BlockSpec((1, 1, D), per_b), pl.BlockSpec((1, 1, D), per_b),
                  pl.BlockSpec((1, D), const),
                  pl.BlockSpec((D, W), const), pl.BlockSpec((1, W), const),
                  pl.BlockSpec((D, NG), const), pl.BlockSpec((1, NG), const)],
        out_specs=[pl.BlockSpec((1, tm, W), tok), pl.BlockSpec((1, tm, NG), tok)],
        out_shape=[jax.ShapeDtypeStruct((B, T, W), F32), jax.ShapeDtypeStruct((B, T, NG), F32)],
        compiler_params=_params("parallel", "parallel"),
        name="mlstm_proj",
    )(x, shift, scale, g, w, b, wg, bg)


def _sgu_fnet_kernel(x_ref, sh_ref, sc_ref, g_ref, w_ref, b_ref, sw_ref, sbt_ref, cc_ref, cs_ref,
                     sgu_ref, a_ref, bm_ref):
    h = _norm_mod(x_ref[0], g_ref[...], sh_ref[0], sc_ref[0])
    p = _dot(h.astype(BF16), w_ref[...]) + b_ref[...]
    tm = p.shape[0]
    u = jax.nn.gelu(p[:, :MIX_W])
    vf = jax.nn.gelu(p[:, MIX_W:2 * MIX_W])
    mu = jnp.mean(vf, axis=-1, keepdims=True)
    var = jnp.mean(jnp.square(vf - mu), axis=-1, keepdims=True)
    v = ((vf - mu) * lax.rsqrt(var + EPS)).astype(BF16)
    for j in range(tm // CHUNK):
        rows = slice(j * CHUNK, (j + 1) * CHUNK)
        for gi in range(N_GROUPS):
            cols = slice(gi * GROUP_W, (gi + 1) * GROUP_W)
            s = _dot(sw_ref[gi], v[rows, cols]) + sbt_ref[:, gi:gi + 1]
            sgu_ref[0, rows, cols] = u[rows, cols] * s
    fz = p[:, 2 * MIX_W:].astype(BF16)
    for gi in range(N_GROUPS):
        cols = slice(gi * GROUP_W, (gi + 1) * GROUP_W)
        a_ref[0, :, cols] = _dot(fz[:, cols], cc_ref[...])
        bm_ref[0, :, cols] = _dot(fz[:, cols], cs_ref[...])


def _sgu_fnet_proj(x, shift, scale, g, w, b, sw, sbt, cc, cs):
    B, T, D = x.shape
    tm = min(512, T)
    W = w.shape[1]
    tok = lambda b_, i: (b_, i, 0)
    per_b = lambda b_, i: (b_, 0, 0)
    c2 = lambda b_, i: (0, 0)
    c3 = lambda b_, i: (0, 0, 0)
    out = jax.ShapeDtypeStruct((B, T, MIX_W), F32)
    return pl.pallas_call(
        _sgu_fnet_kernel,
        grid=(B, T // tm),
        in_specs=[pl.BlockSpec((1, tm, D), tok),
                  pl.BlockSpec((1, 1, D), per_b), pl.BlockSpec((1, 1, D), per_b),
                  pl.BlockSpec((1, D), c2),
                  pl.BlockSpec((D, W), c2), pl.BlockSpec((1, W), c2),
                  pl.BlockSpec((N_GROUPS, CHUNK, CHUNK), c3), pl.BlockSpec((CHUNK, N_GROUPS), c2),
                  pl.BlockSpec((GROUP_W, GROUP_W), c2), pl.BlockSpec((GROUP_W, GROUP_W), c2)],
        out_specs=[pl.BlockSpec((1, tm, MIX_W), tok)] * 3,
        out_shape=[out, out, out],
        compiler_params=_params("parallel", "parallel"),
        name="sgu_fnet_proj",
    )(x, shift, scale, g, w, b, sw, sbt, cc, cs)


def _log_sigmoid(x):
    return jnp.minimum(x, 0.0) - jnp.log1p(jnp.exp(-jnp.abs(x)))


def _mlstm_kernel(q_ref, k_ref, v_ref, halo_ref, gt_ref, gtt_ref, cw_ref, c0_ref, n0_ref, m0_ref,
                  h_ref, c_ref, n_ref, m_ref):
    d = pl.program_id(0)
    c = pl.program_id(2)

    @pl.when(c == 0)
    def _():
        c_ref[0, 0] = c0_ref[0, 0]
        n_ref[0, 0] = n0_ref[0, 0]
        m_ref[0, 0] = m0_ref[0, 0]

    row = lax.broadcasted_iota(jnp.int32, (CHUNK, CHUNK), 0)
    col = lax.broadcasted_iota(jnp.int32, (CHUNK, CHUNK), 1)
    fwd = d == 0
    sgn = jnp.where(fwd, 1, -1)
    mask = (col - row) * sgn <= 0
    tri = mask.astype(F32)
    tri_t = ((row - col) * sgn <= 0).astype(F32)

    W = N_HEADS * HEAD_D
    ridx = lax.broadcasted_iota(jnp.int32, (CHUNK, 2 * W), 0)
    qk = jnp.concatenate([q_ref[0], k_ref[0]], axis=1)
    prev = jnp.where(ridx == 0, halo_ref[0, 0, 0:1, :], pltpu.roll(qk, 1, axis=0))
    nxt = jnp.where(ridx == CHUNK - 1, halo_ref[0, 0, 1:2, :], pltpu.roll(qk, CHUNK - 1, axis=0))
    qk = prev * cw_ref[0:1, :] + qk * cw_ref[1:2, :] + nxt * cw_ref[2:3, :]
    qk = qk * jax.nn.sigmoid(qk)

    gates = gt_ref[0, 0]
    gates_t = gtt_ref[0, 0]
    b_cols = _dot_exact(tri, _log_sigmoid(gates))
    b_rows = _dot_exact(_log_sigmoid(gates_t), tri_t)
    vv = v_ref[0]

    for hd in range(N_HEADS):
        lanes = slice(hd * HEAD_D, (hd + 1) * HEAD_D)
        qf = qk[:, lanes]
        kf = qk[:, W + hd * HEAD_D:W + (hd + 1) * HEAD_D] * (HEAD_D ** -0.5)
        q = qf.astype(BF16)
        v = vv[:, lanes].astype(BF16)
        i_col = gates[:, hd:hd + 1]
        i_row = gates_t[hd:hd + 1, :]
        b_col = b_cols[:, N_HEADS + hd:N_HEADS + hd + 1]
        b_row = b_rows[N_HEADS + hd:N_HEADS + hd + 1, :]
        b_last = jnp.where(fwd, b_row[:, CHUNK - 1:CHUNK], b_row[:, 0:1])

        c_st = c_ref[0, 0, hd]
        n_st = n_ref[0, 0, hd]
        m_st = m_ref[0, 0, hd][:, 0:1]

        dmat = jnp.where(mask, b_col - b_row + i_row, -jnp.inf)
        inter = b_col + m_st
        m_t = jnp.maximum(inter, jnp.max(dmat, axis=-1, keepdims=True))
        s = _dot_nt(q, kf.astype(BF16)) * jnp.exp(dmat - m_t)
        a = jnp.exp(inter - m_t)
        num = _dot(s.astype(BF16), v) + a * _dot(q, c_st.astype(BF16))
        den = jnp.sum(s, axis=-1, keepdims=True) + a * jnp.sum(qf * n_st, axis=-1, keepdims=True)
        h_ref[0, 0, :, lanes] = num / jnp.maximum(jnp.abs(den), jnp.exp(-m_t))

        g_row = b_last - b_row + i_row
        g_col = b_last - b_col + i_col
        m_new = jnp.maximum(b_last + m_st, jnp.max(g_row, axis=-1, keepdims=True))
        kw = kf * jnp.exp(g_col - m_new)
        decay = jnp.exp(b_last + m_st - m_new)
        c_ref[0, 0, hd] = decay * c_st + _dot(jnp.transpose(kw).astype(BF16), v)
        n_ref[0, 0, hd] = decay * n_st + jnp.sum(kw, axis=0, keepdims=True)
        m_ref[0, 0, hd] = jnp.broadcast_to(m_new, (1, HEAD_D))


def _mlstm_scan(qkvo, halo, gates, gates_t, conv_w, c0, n0, m0):
    B, T, _ = qkvo.shape
    nc = T // CHUNK
    W = N_HEADS * HEAD_D

    def chunk(d, c):
        return c + d * (nc - 1 - 2 * c)

    st = lambda d, b, c: (d, b, 0, 0, 0)
    return pl.pallas_call(
        _mlstm_kernel,
        grid=(2, B, nc),
        in_specs=[pl.BlockSpec((1, CHUNK, W), lambda d, b, c: (b, chunk(d, c), 0)),
                  pl.BlockSpec((1, CHUNK, W), lambda d, b, c: (b, chunk(d, c), 1)),
                  pl.BlockSpec((1, CHUNK, W), lambda d, b, c: (b, chunk(d, c), 2)),
                  pl.BlockSpec((1, 1, 2, 2 * W), lambda d, b, c: (b, chunk(d, c), 0, 0)),
                  pl.BlockSpec((1, 1, CHUNK, 128), lambda d, b, c: (d, b, chunk(d, c), 0)),
                  pl.BlockSpec((1, 1, 2 * N_HEADS, CHUNK), lambda d, b, c: (d, b, 0, chunk(d, c))),
                  pl.BlockSpec((3, 2 * W), lambda d, b, c: (0, 0)),
                  pl.BlockSpec((1, 1, N_HEADS, HEAD_D, HEAD_D), st),
                  pl.BlockSpec((1, 1, N_HEADS, 1, HEAD_D), st),
                  pl.BlockSpec((1, 1, N_HEADS, 1, HEAD_D), st)],
        out_specs=[pl.BlockSpec((1, 1, CHUNK, W), lambda d, b, c: (d, b, chunk(d, c), 0)),
                   pl.BlockSpec((1, 1, N_HEADS, HEAD_D, HEAD_D), st),
                   pl.BlockSpec((1, 1, N_HEADS, 1, HEAD_D), st),
                   pl.BlockSpec((1, 1, N_HEADS, 1, HEAD_D), st)],
        out_shape=[jax.ShapeDtypeStruct((2, B, T, W), F32),
                   jax.ShapeDtypeStruct((2, B, N_HEADS, HEAD_D, HEAD_D), F32),
                   jax.ShapeDtypeStruct((2, B, N_HEADS, 1, HEAD_D), F32),
                   jax.ShapeDtypeStruct((2, B, N_HEADS, 1, HEAD_D), F32)],
        compiler_params=_params("parallel", "parallel", "arbitrary"),
        name="mlstm_scan",
    )(qkvo, qkvo, qkvo, halo, gates, gates_t, conv_w, c0, n0, m0)


def _dft_mats(n):
    j = np.arange(n, dtype=np.float64)
    ang = 2.0 * np.pi * np.outer(j, j) / n
    return np.cos(ang) / np.sqrt(n), np.sin(ang) / np.sqrt(n)


def _fft_stage1_kernel(a_ref, b_ref, w_ref, o_ref):
    ab = jnp.concatenate([a_ref[0], b_ref[0]], axis=0).astype(BF16)
    o_ref[0] = _dot(w_ref[...], ab)


def _fft_stage2_kernel(x_ref, tc_ref, ts_ref, w_ref, o_ref):
    nk = x_ref.shape[2]
    for j in range(nk):
        xr = x_ref[0, 0, j]
        xi = x_ref[0, 1, j]
        cw = jnp.concatenate([tc_ref[j]] * N_GROUPS, axis=1)
        sw = jnp.concatenate([ts_ref[j]] * N_GROUPS, axis=1)
        yr = xr * cw + xi * sw
        yi = xi * cw - xr * sw
        y = jnp.concatenate([yr, yi], axis=0).astype(BF16)
        o_ref[0, :, j, :] = _dot(w_ref[...], y)


def _fnet_seq_dft_long(a, b):
    B, T, W = a.shape
    T2 = CHUNK
    T1 = T // T2
    c1, s1 = _dft_mats(T1)
    c2, s2 = _dft_mats(T2)
    w1 = jnp.asarray(np.block([[c1, -s1], [-s1, -c1]]), BF16)
    w2 = jnp.asarray(np.concatenate([c2, s2], axis=1), BF16)
    ang = (2.0 * np.pi / T) * np.outer(np.arange(T1), np.arange(T2))
    tw_c = jnp.broadcast_to(jnp.asarray(np.cos(ang), F32)[:, :, None], (T1, T2, GROUP_W))
    tw_s = jnp.broadcast_to(jnp.asarray(np.sin(ang), F32)[:, :, None], (T1, T2, GROUP_W))

    ncol = T2 * W
    cb = ncol // 4
    x1 = pl.pallas_call(
        _fft_stage1_kernel,
        grid=(B, 4),
        in_specs=[pl.BlockSpec((1, T1, cb), lambda b_, j: (b_, 0, j)),
                  pl.BlockSpec((1, T1, cb), lambda b_, j: (b_, 0, j)),
                  pl.BlockSpec((2 * T1, 2 * T1), lambda b_, j: (0, 0))],
        out_specs=pl.BlockSpec((1, 2 * T1, cb), lambda b_, j: (b_, 0, j)),
        out_shape=jax.ShapeDtypeStruct((B, 2 * T1, ncol), F32),
        compiler_params=_params("parallel", "parallel"),
        name="fnet_dft_stage1",
    )(a.reshape(B, T1, ncol), b.reshape(B, T1, ncol), w1)

    nk = min(8, T1)
    y = pl.pallas_call(
        _fft_stage2_kernel,
        grid=(B, T1 // nk),
        in_specs=[pl.BlockSpec((1, 2, nk, T2, W), lambda b_, j: (b_, 0, j, 0, 0)),
                  pl.BlockSpec((nk, T2, GROUP_W), lambda b_, j: (j, 0, 0)),
                  pl.BlockSpec((nk, T2, GROUP_W), lambda b_, j: (j, 0, 0)),
                  pl.BlockSpec((T2, 2 * T2), lambda b_, j: (0, 0))],
        out_specs=pl.BlockSpec((1, T2, nk, W), lambda b_, j: (b_, 0, j, 0)),
        out_shape=jax.ShapeDtypeStruct((B, T2, T1, W), F32),
        compiler_params=_params("parallel", "parallel"),
        name="fnet_dft_stage2",
    )(x1.reshape(B, 2, T1, T2, W), tw_c, tw_s, w2)
    return y.reshape(B, T, W)


def _dft_short_kernel(a_ref, b_ref, w_ref, o_ref):
    ab = jnp.concatenate([a_ref[0], b_ref[0]], axis=0).astype(BF16)
    o_ref[0] = _dot(w_ref[...], ab)


def _fnet_seq_dft_short(a, b):
    B, T, W = a.shape
    c, s = _dft_mats(T)
    w = jnp.asarray(np.concatenate([c, -s], axis=1), BF16)
    return pl.pallas_call(
        _dft_short_kernel,
        grid=(B,),
        in_specs=[pl.BlockSpec((1, T, W), lambda b_: (b_, 0, 0)),
                  pl.BlockSpec((1, T, W), lambda b_: (b_, 0, 0)),
                  pl.BlockSpec((T, 2 * T), lambda b_: (0, 0))],
        out_specs=pl.BlockSpec((1, T, W), lambda b_: (b_, 0, 0)),
        out_shape=jax.ShapeDtypeStruct((B, T, W), F32),
        compiler_params=_params("parallel"),
        name="fnet_dft_short",
    )(a, b, w)


def _merge_kernel(x_ref, sh_ref, sc_ref, gt_ref, g_ref, hf_ref, hb_ref, o_ref, mg_ref, sgu_ref, fn_ref,
                  wmg_ref, bmg_ref, wbr_ref, wout_ref, out_ref):
    x = x_ref[0]
    h = _norm_mod(x, g_ref[...], sh_ref[0], sc_ref[0]).astype(BF16)
    hs = hf_ref[0, 0] + hb_ref[0, 0]
    parts = []
    for hd in range(N_HEADS):
        z = hs[:, hd * HEAD_D:(hd + 1) * HEAD_D]
        mu = jnp.mean(z, axis=-1, keepdims=True)
        var = jnp.mean(jnp.square(z - mu), axis=-1, keepdims=True)
        parts.append((z - mu) * lax.rsqrt(var + EPS))
    hn = jnp.concatenate(parts, axis=1) * mg_ref[...]
    ym = jax.nn.sigmoid(o_ref[0]) * hn
    y = None
    for r, br in enumerate((ym, sgu_ref[0], fn_ref[0])):
        cols = slice(r * D_MODEL, (r + 1) * D_MODEL)
        gate = jax.nn.sigmoid(_dot(h, wmg_ref[:, cols]) + bmg_ref[:, cols])
        term = gate * _dot(br.astype(BF16), wbr_ref[r])
        y = term if y is None else y + term
    out_ref[0] = x + gt_ref[0] * _dot(y.astype(BF16), wout_ref[...])


def _merge(x, shift, scale, gate, g, hdir, qkvo, mnorm_g, sgu, fnet, wmg, bmg, wbr, wout):
    B, T, D = x.shape
    tm = min(256, T)
    tok = lambda b_, i: (b_, i, 0)
    per_b = lambda b_, i: (b_, 0, 0)
    c2 = lambda b_, i: (0, 0)
    return pl.pallas_call(
        _merge_kernel,
        grid=(B, T // tm),
        in_specs=[pl.BlockSpec((1, tm, D), tok),
                  pl.BlockSpec((1, 1, D), per_b), pl.BlockSpec((1, 1, D), per_b), pl.BlockSpec((1, 1, D), per_b),
                  pl.BlockSpec((1, D), c2),
                  pl.BlockSpec((1, 1, tm, MIX_W), lambda b_, i: (0, b_, i, 0)),
                  pl.BlockSpec((1, 1, tm, MIX_W), lambda b_, i: (1, b_, i, 0)),
                  pl.BlockSpec((1, tm, MIX_W), lambda b_, i: (b_, i, 3)),
                  pl.BlockSpec((1, MIX_W), c2),
                  pl.BlockSpec((1, tm, MIX_W), tok), pl.BlockSpec((1, tm, MIX_W), tok),
                  pl.BlockSpec((D, 3 * D), c2), pl.BlockSpec((1, 3 * D), c2),
                  pl.BlockSpec((3, MIX_W, D), lambda b_, i: (0, 0, 0)),
                  pl.BlockSpec((D, D), c2)],
        out_specs=pl.BlockSpec((1, tm, D), tok),
        out_shape=jax.ShapeDtypeStruct(x.shape, F32),
        compiler_params=_params("parallel", "parallel"),
        name="branch_merge",
    )(x, shift, scale, gate, g, hdir, hdir, qkvo, mnorm_g, sgu, fnet, wmg, bmg, wbr, wout)


def _top_rows(s, k):
    n = s.shape[1]
    row = lax.broadcasted_iota(jnp.int32, s.shape, 0).astype(F32)
    slot = lax.broadcasted_iota(jnp.int32, (k, n), 0)
    vals = jnp.zeros((k, n), F32)
    ids = jnp.zeros((k, n), F32)
    for i in range(k):
        m = jnp.max(s, axis=0, keepdims=True)
        idx = jnp.min(jnp.where(s == m, row, float(s.shape[0])), axis=0, keepdims=True)
        vals = jnp.where(slot == i, m, vals)
        ids = jnp.where(slot == i, idx, ids)
        s = jnp.where(row == idx, -jnp.inf, s)
    return vals, ids


def _peer_topk_kernel(x_ref, sh_ref, sc_ref, g_ref, wq_ref, keys_ref, h2_ref, idx_ref, w_ref,
                      q_scr, idx_scr, w_scr):
    h2 = _norm_mod(x_ref[0], g_ref[...], sh_ref[0], sc_ref[0])
    h2_ref[0] = h2
    q_scr[...] = _dot(h2.astype(BF16), wq_ref[...])
    tm = h2.shape[0]
    n_half = TOPK // 2
    n_cand = TOPK + (TOPK - 1) * n_half

    def head(hd, carry):
        tops = []
        for p in range(2):
            off = pl.multiple_of(hd * (2 * N_KEYS) + p * N_KEYS, N_KEYS)
            qhp = q_scr[:, pl.ds(off, N_KEYS)].astype(BF16)
            tops.append(_top_rows(_dot_nt(keys_ref[p], qhp), TOPK))
        (va, ia), (vb, ib) = tops
        cs = [va[0:1] + vb]
        ce = [ia[0:1] * N_KEYS + ib]
        for i in range(1, TOPK):
            cs.append(va[i:i + 1] + vb[0:n_half])
            ce.append(ia[i:i + 1] * N_KEYS + ib[0:n_half])
        cs = jnp.concatenate(cs, axis=0)
        ce = jnp.concatenate(ce, axis=0)
        row = lax.broadcasted_iota(jnp.int32, (n_cand, tm), 0).astype(F32)
        slot = lax.broadcasted_iota(jnp.int32, (TOPK, tm), 0)
        best = jnp.zeros((TOPK, tm), F32)
        sel = jnp.zeros((TOPK, tm), F32)
        for i in range(TOPK):
            m = jnp.max(cs, axis=0, keepdims=True)
            idx = jnp.min(jnp.where(cs == m, row, float(n_cand)), axis=0, keepdims=True)
            hit = row == idx
            e = jnp.max(jnp.where(hit, ce, -1.0), axis=0, keepdims=True)
            best = jnp.where(slot == i, m, best)
            sel = jnp.where(slot == i, e, sel)
            cs = jnp.where(hit, -jnp.inf, cs)
        ex = jnp.exp(best - best[0:1])
        wts = ex / jnp.sum(ex, axis=0, keepdims=True)
        r0 = pl.multiple_of(hd * TOPK, TOPK)
        idx_scr[pl.ds(r0, TOPK), :] = sel
        w_scr[pl.ds(r0, TOPK), :] = wts
        return carry

    lax.fori_loop(0, PEER_HEADS, head, 0)
    idx_ref[0] = (idx_scr[...].T).astype(jnp.int32) * SLAB
    w_ref[0] = w_scr[...].T


def _peer_topk(x, shift, scale, g, wq, keys):
    B, T, D = x.shape
    tm = min(256, T)
    tok = lambda b_, i: (b_, i, 0)
    per_b = lambda b_, i: (b_, 0, 0)
    c2 = lambda b_, i: (0, 0)
    WQ = wq.shape[1]
    return pl.pallas_call(
        _peer_topk_kernel,
        grid=(B, T // tm),
        in_specs=[pl.BlockSpec((1, tm, D), tok),
                  pl.BlockSpec((1, 1, D), per_b), pl.BlockSpec((1, 1, D), per_b),
                  pl.BlockSpec((1, D), c2),
                  pl.BlockSpec((D, WQ), c2),
                  pl.BlockSpec((2, N_KEYS, N_KEYS), lambda b_, i: (0, 0, 0))],
        out_specs=[pl.BlockSpec((1, tm, D), tok), pl.BlockSpec((1, tm, N_SEL), tok),
                   pl.BlockSpec((1, tm, N_SEL), tok)],
        out_shape=[jax.ShapeDtypeStruct((B, T, D), F32),
                   jax.ShapeDtypeStruct((B, T, N_SEL), jnp.int32),
                   jax.ShapeDtypeStruct((B, T, N_SEL), F32)],
        scratch_shapes=[pltpu.VMEM((tm, WQ), F32), pltpu.VMEM((N_SEL, tm), F32), pltpu.VMEM((N_SEL, tm), F32)],
        compiler_params=_params("parallel", "parallel"),
        name="peer_topk",
    )(x, shift, scale, g, wq, keys)


def _pack_table(tab):
    bits = lax.bitcast_convert_type(tab.astype(TABLE_DT), jnp.uint16).astype(jnp.uint32)
    word = bits[:, :HALF_D] | (bits[:, HALF_D:] << 16)
    return word.reshape(tab.shape[0] * SLAB, 128)


def _gather_rows(idx_ref, t, tab_ref, g_ref):
    for m in range(N_SEL):
        e = pl.multiple_of(idx_ref[t, m], SLAB)
        g_ref[pl.ds(m * SLAB, SLAB), :] = tab_ref[pl.ds(e, SLAB), :]
    chunks = [g_ref[pl.ds(c, N_SEL, stride=SLAB), :] for c in range(SLAB)]
    return pltpu.bitcast(jnp.concatenate(chunks, axis=1), TABLE_DT)


def _two_rows(r0, r1, width):
    sub = lax.broadcasted_iota(jnp.int32, (8, width), 0)
    return jnp.where(sub == 0, r0, jnp.where(sub == 1, r1, 0.0))


def _peer_u_kernel(idx_ref, h2_ref, w_ref, ex_ref, tab_ref, a_ref, g0_ref, g1_ref, act_ref):
    tm = h2_ref.shape[0]
    lane = lax.broadcasted_iota(jnp.int32, (8, 2 * N_SEL), 1)
    even = (lane & 1) == 0
    sub = lax.broadcasted_iota(jnp.int32, (8, 2 * N_SEL), 0)

    def group(gi, carry):
        base = pl.multiple_of(gi * 8, 8)
        xg = h2_ref[pl.ds(base, 8), :]
        rows = jnp.zeros((8, 2 * N_SEL), F32)
        for j in range(8):
            g = _gather_rows(idx_ref, base + j, tab_ref, g0_ref if j % 2 == 0 else g1_ref)
            lhs = _two_rows(xg[j:j + 1, :HALF_D], xg[j:j + 1, HALF_D:], HALF_D).astype(TABLE_DT)
            out = _dot_nt(lhs, g)
            z = jnp.where(even, out, pltpu.roll(out, 7, axis=0))
            s = z + pltpu.roll(z, 2 * N_SEL - 1, axis=1)
            rows = jnp.where(sub == j, s[0:1], rows)
        act_ref[pl.ds(base, 8), :] = rows
        return carry

    lax.fori_loop(0, tm // 8, group, 0)
    w2 = _dot_exact(w_ref[...], ex_ref[...])
    a_ref[...] = jax.nn.gelu(act_ref[...]) * w2


def _peer_v_kernel(idx_ref, a_ref, x_ref, gt_ref, tab_ref, o_ref, g0_ref, g1_ref):
    tm = a_ref.shape[0]
    sub = lax.broadcasted_iota(jnp.int32, (8, D_MODEL), 0)

    def group(gi, carry):
        base = pl.multiple_of(gi * 8, 8)
        ag = a_ref[pl.ds(base, 8), :]
        rows = jnp.zeros((8, D_MODEL), F32)
        for j in range(8):
            g = _gather_rows(idx_ref, base + j, tab_ref, g0_ref if j % 2 == 0 else g1_ref)
            a0 = jnp.broadcast_to(ag[j:j + 1, :], (8, 2 * N_SEL))
            lhs = _two_rows(a0, pltpu.roll(a0, 1, axis=1), 2 * N_SEL).astype(TABLE_DT)
            out = _dot(lhs, g)
            rows = jnp.where(sub == j, jnp.concatenate([out[0:1], out[1:2]], axis=1), rows)
        o_ref[pl.ds(base, 8), :] = x_ref[pl.ds(base, 8), :] + gt_ref[0] * rows
        return carry

    lax.fori_loop(0, tm // 8, group, 0)


def _peer_expert_mix(x, gate, h2, idx, w, utab, vtab):
    B, T, D = x.shape
    N = B * T
    tm = 64
    tpb = T // tm
    expand = np.zeros((N_SEL, 2 * N_SEL), np.float32)
    expand[np.arange(N_SEL), 2 * np.arange(N_SEL)] = 1.0
    flat = lambda i: (i, 0)
    smem = functools.partial(pl.BlockSpec, memory_space=pltpu.SMEM)
    idx2 = idx.reshape(N, N_SEL)
    gscr = pltpu.VMEM((N_SEL * SLAB, 128), jnp.uint32)
    a = pl.pallas_call(
        _peer_u_kernel,
        grid=(N // tm,),
        in_specs=[smem((tm, N_SEL), flat),
                  pl.BlockSpec((tm, D), flat),
                  pl.BlockSpec((tm, N_SEL), flat),
                  pl.BlockSpec((N_SEL, 2 * N_SEL), lambda i: (0, 0)),
                  pl.BlockSpec(memory_space=pltpu.VMEM)],
        out_specs=pl.BlockSpec((tm, 2 * N_SEL), flat),
        out_shape=jax.ShapeDtypeStruct((N, 2 * N_SEL), F32),
        scratch_shapes=[gscr, gscr, pltpu.VMEM((tm, 2 * N_SEL), F32)],
        compiler_params=_params("parallel"),
        name="peer_expert_in",
    )(idx2, h2.reshape(N, D), w.reshape(N, N_SEL), jnp.asarray(expand), utab)
    out = pl.pallas_call(
        _peer_v_kernel,
        grid=(N // tm,),
        in_specs=[smem((tm, N_SEL), flat),
                  pl.BlockSpec((tm, 2 * N_SEL), flat),
                  pl.BlockSpec((tm, D), flat),
                  pl.BlockSpec((1, 1, D), lambda i: (i // tpb, 0, 0)),
                  pl.BlockSpec(memory_space=pltpu.VMEM)],
        out_specs=pl.BlockSpec((tm, D), flat),
        out_shape=jax.ShapeDtypeStruct((N, D), F32),
        scratch_shapes=[gscr, gscr],
        compiler_params=_params("parallel"),
        name="peer_expert_out",
    )(idx2, a, x.reshape(N, D), gate, vtab)
    return out.reshape(B, T, D)


def _halo_rows(qk):
    B, T, C = qk.shape
    z = jnp.zeros((B, 1, C), qk.dtype)
    prev = jnp.concatenate([z, qk[:, CHUNK - 1:T - 1:CHUNK]], axis=1)
    nxt = jnp.concatenate([qk[:, CHUNK::CHUNK], z], axis=1)
    return jnp.stack([prev, nxt], axis=2)


def _stream_layer(x, mod, lw, init):
    B, T, D = x.shape
    sh1, sc1, gt1, sh2, sc2, gt2 = mod
    qkvo, gates = _mlstm_proj(x, sh1, sc1, lw["g1"], lw["w_qkvo"], lw["b_qkvo"], lw["w_gate"], lw["b_gate"])
    sgu, fa, fb = _sgu_fnet_proj(x, sh1, sc1, lw["g1"], lw["w_sf"], lw["b_sf"], lw["sgu_w"], lw["sgu_bt"],
                                 lw["dft_c"], lw["dft_s"])
    halo = _halo_rows(qkvo[:, :, :2 * MIX_W])
    gd = jnp.transpose(gates[:, :, :4 * N_HEADS].reshape(B, T, 2, 2 * N_HEADS), (2, 0, 1, 3))
    gd_pad = jnp.pad(gd, ((0, 0), (0, 0), (0, 0), (0, 128 - 2 * N_HEADS)))
    hdir, c_fin, n_fin, m_fin = _mlstm_scan(qkvo, halo, gd_pad, jnp.swapaxes(gd, 2, 3), lw["conv"], *init)
    fnet = _fnet_seq_dft_long(fa, fb) if T > 2 * CHUNK else _fnet_seq_dft_short(fa, fb)
    x = _merge(x, sh1, sc1, gt1, lw["g1"], hdir, qkvo, lw["mnorm_g"], sgu, fnet,
               lw["w_mg"], lw["b_mg"], lw["w_br"], lw["w_out"])
    h2, idx, w = _peer_topk(x, sh2, sc2, lw["g2"], lw["wq"], lw["keys"])
    x = _peer_expert_mix(x, gt2, h2, idx, w, lw["utab"], lw["vtab"])
    return x, (c_fin, n_fin, m_fin)


def _pos_embed(rows):
    quarter = D_MODEL // 4
    omega = 1.0 / (10000.0 ** (jnp.arange(quarter, dtype=F32) / quarter))
    r = jnp.repeat(jnp.arange(rows, dtype=F32), GRID_W)[:, None] * omega
    cc = jnp.tile(jnp.arange(GRID_W, dtype=F32), rows)[:, None] * omega
    return jnp.concatenate([jnp.sin(r), jnp.cos(r), jnp.sin(cc), jnp.cos(cc)], axis=-1)


def kernel(x, c, ctx, c_ctx, w_mod, b_mod, norm1_g, norm2_g, w_in, b_in, conv_qk, mlstm_norm_g, sgu_w, sgu_b, w_br, w_out, peer_wq, peer_keys, peer_u, peer_v, final_g):
    B, T, D = x.shape
    depth = w_mod.shape[0]
    x = _add_pos(x, _pos_embed(T // GRID_W))

    cc = jnp.zeros((8, D), F32).at[:B].set(c).at[B].set(c_ctx)
    mods = _modulation(cc, w_mod, b_mod)

    cdft, sdft = _dft_mats(GROUP_W)
    dft_c = jnp.asarray(cdft, BF16)
    dft_s = jnp.asarray(sdft, BF16)
    o_q, o_g = 4 * MIX_W, 4 * MIX_W + 4 * N_HEADS
    o_sf, o_mg = o_g, o_g + 3 * MIX_W

    def layer(carry, lp):
        x, ctx = carry
        (mod, g1, g2, win, bin_, conv, mng, sw, sb, wbr, wout, wq, keys, pu, pv) = lp
        lw = {
            "g1": g1.reshape(1, D), "g2": g2.reshape(1, D),
            "w_qkvo": win[:, :o_q].astype(BF16), "b_qkvo": bin_[:o_q].reshape(1, -1),
            "w_gate": jnp.pad(win[:, o_q:o_g], ((0, 0), (0, 128 - 4 * N_HEADS))),
            "b_gate": jnp.pad(bin_[o_q:o_g], (0, 128 - 4 * N_HEADS)).reshape(1, -1),
            "w_sf": win[:, o_sf:o_mg].astype(BF16), "b_sf": bin_[o_sf:o_mg].reshape(1, -1),
            "w_mg": win[:, o_mg:].astype(BF16), "b_mg": bin_[o_mg:].reshape(1, -1),
            "conv": conv, "mnorm_g": mng.reshape(1, -1),
            "sgu_w": sw.astype(BF16), "sgu_bt": sb.T,
            "dft_c": dft_c, "dft_s": dft_s,
            "w_br": wbr.astype(BF16), "w_out": wout.astype(BF16),
            "wq": wq.astype(BF16), "keys": keys.astype(BF16),
            "utab": _pack_table(pu), "vtab": _pack_table(pv),
        }
        mod_x = [mod[:B, i * D:(i + 1) * D].reshape(B, 1, D) for i in range(N_MOD)]
        mod_c = [jnp.broadcast_to(mod[B, i * D:(i + 1) * D].reshape(1, 1, D), (B, 1, D)) for i in range(N_MOD)]
        zero = (jnp.zeros((2, B, N_HEADS, HEAD_D, HEAD_D), F32),
                jnp.zeros((2, B, N_HEADS, 1, HEAD_D), F32),
                jnp.zeros((2, B, N_HEADS, 1, HEAD_D), F32))
        ctx_new, ctx_state = _stream_layer(ctx, mod_c, lw, zero)
        x_new, _ = _stream_layer(x, mod_x, lw, ctx_state)
        return (x_new, ctx_new), None

    xs = (mods, norm1_g, norm2_g, w_in, b_in, conv_qk, mlstm_norm_g, sgu_w, sgu_b, w_br, w_out,
          peer_wq, peer_keys, peer_u, peer_v)
    (x, ctx), _ = lax.scan(layer, (x, ctx), xs)
    return _final_norm(x, final_g)
```

```python
import numpy as np
import jax
import jax.numpy as jnp
from jax import lax
from jax.experimental import pallas as pl
from jax.experimental.pallas import tpu as pltpu

F32 = jnp.float32
BF16 = jnp.bfloat16
TABLE_DT = jnp.bfloat16
HIGHEST = lax.Precision.HIGHEST

D_MODEL = 1024
GRID_W = 64
EPS = 1e-6
N_MOD = 6
N_HEADS = 4
HEAD_D = 128
CHUNK = 128
MIX_W = 512
N_GROUPS = 4
GROUP_W = 128
PEER_HEADS = 8
N_KEYS = 128
TOPK = 16
N_SEL = PEER_HEADS * TOPK
HALF_D = D_MODEL // 2
SLAB = 4
GROUP = 8
STAGE = 32
VMEM_LIMIT = 56 * 1024 * 1024


def _params(*sem):
    return pltpu.CompilerParams(dimension_semantics=sem, vmem_limit_bytes=VMEM_LIMIT)


def _dot(a, b):
    return jnp.dot(a, b, preferred_element_type=F32)


def _dot_nt(a, b):
    return lax.dot_general(a, b, (((1,), (1,)), ((), ())), preferred_element_type=F32)


def _dot_tn(a, b):
    return lax.dot_general(a, b, (((0,), (0,)), ((), ())), preferred_element_type=F32)


def _dot_exact(a, b):
    return jnp.dot(a, b, preferred_element_type=F32, precision=HIGHEST)


def _norm_mod(x, g, shift, scale):
    y = x * lax.rsqrt(jnp.mean(x * x, axis=-1, keepdims=True) + EPS)
    return (y * g) * (1.0 + scale) + shift


def _add_pos_kernel(x_ref, p_ref, o_ref):
    o_ref[0] = x_ref[0] + p_ref[...]


def _add_pos(x, pos):
    B, T, D = x.shape
    tm = 512
    return pl.pallas_call(
        _add_pos_kernel,
        grid=(B, T // tm),
        in_specs=[pl.BlockSpec((1, tm, D), lambda b, i: (b, i, 0)),
                  pl.BlockSpec((tm, D), lambda b, i: (i, 0))],
        out_specs=pl.BlockSpec((1, tm, D), lambda b, i: (b, i, 0)),
        out_shape=jax.ShapeDtypeStruct(x.shape, x.dtype),
        compiler_params=_params("parallel", "parallel"),
        name="add_pos",
    )(x, pos)


def _final_norm_kernel(x_ref, g_ref, o_ref):
    x = x_ref[0]
    o_ref[0] = x * lax.rsqrt(jnp.mean(x * x, axis=-1, keepdims=True) + EPS) * g_ref[...]


def _final_norm(x, g):
    B, T, D = x.shape
    tm = 512
    return pl.pallas_call(
        _final_norm_kernel,
        grid=(B, T // tm),
        in_specs=[pl.BlockSpec((1, tm, D), lambda b, i: (b, i, 0)),
                  pl.BlockSpec((1, D), lambda b, i: (0, 0))],
        out_specs=pl.BlockSpec((1, tm, D), lambda b, i: (b, i, 0)),
        out_shape=jax.ShapeDtypeStruct(x.shape, x.dtype),
        compiler_params=_params("parallel", "parallel"),
        name="final_norm",
    )(x, g.reshape(1, D))


def _mod_kernel(c_ref, w_ref, b_ref, o_ref):
    c = c_ref[...]
    s = c * jax.nn.sigmoid(c)
    o_ref[0] = _dot_exact(s, w_ref[0]) + b_ref[0]


def _modulation(cc, w_mod, b_mod):
    depth, D, W = w_mod.shape
    return pl.pallas_call(
        _mod_kernel,
        grid=(depth, W // D),
        in_specs=[pl.BlockSpec((8, D), lambda l, j: (0, 0)),
                  pl.BlockSpec((1, D, D), lambda l, j: (l, 0, j)),
                  pl.BlockSpec((1, 1, D), lambda l, j: (l, 0, j))],
        out_specs=pl.BlockSpec((1, 8, D), lambda l, j: (l, 0, j)),
        out_shape=jax.ShapeDtypeStruct((depth, 8, W), F32),
        compiler_params=_params("parallel", "parallel"),
        name="adaln_mod",
    )(cc, w_mod, b_mod.reshape(depth, 1, W))


def _mproj_kernel(x_ref, sh_ref, sc_ref, g_ref, w_ref, b_ref, wg_ref, bg_ref, o_ref, og_ref):
    h = _norm_mod(x_ref[0], g_ref[...], sh_ref[0], sc_ref[0])
    o_ref[0] = _dot(h.astype(BF16), w_ref[...]) + b_ref[...]
    og_ref[0] = _dot_exact(h, wg_ref[...]) + bg_ref[...]


def _mlstm_proj(x, shift, scale, g, w, b, wg, bg):
    B, T, D = x.shape
    tm = min(512, T)
    W = w.shape[1]
    NG = wg.shape[1]
    tok = lambda b_, i: (b_, i, 0)
    per_b = lambda b_, i: (b_, 0, 0)
    const = lambda b_, i: (0, 0)
    return pl.pallas_call(
        _mproj_kernel,
        grid=(B, T // tm),
        in_specs=[pl.BlockSpec((1, tm, D), tok),
                  pl.BlockSpec((1, 1, D), per_b), pl.BlockSpec((1, 1, D), per_b),
                  pl.BlockSpec((1, D), const),
                  pl.BlockSpec((D, W), const), pl.BlockSpec((1, W), const),
                  pl.BlockSpec((D, NG), const), pl.BlockSpec((1, NG), const)],
        out_specs=[pl.BlockSpec((1, tm, W), tok), pl.BlockSpec((1, tm, NG), tok)],
        out_shape=[jax.ShapeDtypeStruct((B, T, W), F32), jax.ShapeDtypeStruct((B, T, NG), F32)],
        compiler_params=_params("parallel", "parallel"),
        name="mlstm_proj",
    )(x, shift, scale, g, w, b, wg, bg)


def _sgu_fnet_kernel(x_ref, sh_ref, sc_ref, g_ref, w_ref, b_ref, sw_ref, sbt_ref, cc_ref, cs_ref,
                     sgu_ref, a_ref, bm_ref):
    h = _norm_mod(x_ref[0], g_ref[...], sh_ref[0], sc_ref[0])
    p = _dot(h.astype(BF16), w_ref[...]) + b_ref[...]
    tm = p.shape[0]
    u = jax.nn.gelu(p[:, :MIX_W])
    vf = jax.nn.gelu(p[:, MIX_W:2 * MIX_W])
    mu = jnp.mean(vf, axis=-1, keepdims=True)
    var = jnp.mean(jnp.square(vf - mu), axis=-1, keepdims=True)
    v = ((vf - mu) * lax.rsqrt(var + EPS)).astype(BF16)
    for j in range(tm // CHUNK):
        rows = slice(j * CHUNK, (j + 1) * CHUNK)
        for gi in range(N_GROUPS):
            cols = slice(gi * GROUP_W, (gi + 1) * GROUP_W)
            s = _dot(sw_ref[gi], v[rows, cols]) + sbt_ref[:, gi:gi + 1]
            sgu_ref[0, rows, cols] = u[rows, cols] * s
    fz = p[:, 2 * MIX_W:].astype(BF16)
    for gi in range(N_GROUPS):
        cols = slice(gi * GROUP_W, (gi + 1) * GROUP_W)
        a_ref[0, :, cols] = _dot(fz[:, cols], cc_ref[...])
        bm_ref[0, :, cols] = _dot(fz[:, cols], cs_ref[...])


def _sgu_fnet_proj(x, shift, scale, g, w, b, sw, sbt, cc, cs):
    B, T, D = x.shape
    tm = min(512, T)
    W = w.shape[1]
    tok = lambda b_, i: (b_, i, 0)
    per_b = lambda b_, i: (b_, 0, 0)
    c2 = lambda b_, i: (0, 0)
    c3 = lambda b_, i: (0, 0, 0)
    out = jax.ShapeDtypeStruct((B, T, MIX_W), F32)
    return pl.pallas_call(
        _sgu_fnet_kernel,
        grid=(B, T // tm),
        in_specs=[pl.BlockSpec((1, tm, D), tok),
                  pl.BlockSpec((1, 1, D), per_b), pl.BlockSpec((1, 1, D), per_b),
                  pl.BlockSpec((1, D), c2),
                  pl.BlockSpec((D, W), c2), pl.BlockSpec((1, W), c2),
                  pl.BlockSpec((N_GROUPS, CHUNK, CHUNK), c3), pl.BlockSpec((CHUNK, N_GROUPS), c2),
                  pl.BlockSpec((GROUP_W, GROUP_W), c2), pl.BlockSpec((GROUP_W, GROUP_W), c2)],
        out_specs=[pl.BlockSpec((1, tm, MIX_W), tok)] * 3,
        out_shape=[out, out, out],
        compiler_params=_params("parallel", "parallel"),
        name="sgu_fnet_proj",
    )(x, shift, scale, g, w, b, sw, sbt, cc, cs)


def _log_sigmoid(x):
    return jnp.minimum(x, 0.0) - jnp.log1p(jnp.exp(-jnp.abs(x)))


def _mlstm_kernel(q_ref, k_ref, v_ref, halo_ref, gt_ref, gtt_ref, cw_ref, c0_ref, n0_ref, m0_ref,
                  h_ref, c_ref, n_ref, m_ref):
    d = pl.program_id(0)
    c = pl.program_id(2)

    @pl.when(c == 0)
    def _():
        c_ref[0, 0] = c0_ref[0, 0]
        n_ref[0, 0] = n0_ref[0, 0]
        m_ref[0, 0] = m0_ref[0, 0]

    row = lax.broadcasted_iota(jnp.int32, (CHUNK, CHUNK), 0)
    col = lax.broadcasted_iota(jnp.int32, (CHUNK, CHUNK), 1)
    fwd = d == 0
    sgn = jnp.where(fwd, 1, -1)
    mask = (col - row) * sgn <= 0
    tri = mask.astype(F32)
    tri_t = ((row - col) * sgn <= 0).astype(F32)

    W = N_HEADS * HEAD_D
    ridx = lax.broadcasted_iota(jnp.int32, (CHUNK, 2 * W), 0)
    qk = jnp.concatenate([q_ref[0], k_ref[0]], axis=1)
    prev = jnp.where(ridx == 0, halo_ref[0, 0, 0:1, :], pltpu.roll(qk, 1, axis=0))
    nxt = jnp.where(ridx == CHUNK - 1, halo_ref[0, 0, 1:2, :], pltpu.roll(qk, CHUNK - 1, axis=0))
    qk = prev * cw_ref[0:1, :] + qk * cw_ref[1:2, :] + nxt * cw_ref[2:3, :]
    qk = qk * jax.nn.sigmoid(qk)

    gates = gt_ref[0, 0]
    gates_t = gtt_ref[0, 0]
    b_cols = _dot_exact(tri, _log_sigmoid(gates))
    b_rows = _dot_exact(_log_sigmoid(gates_t), tri_t)
    vv = v_ref[0]

    for hd in range(N_HEADS):
        lanes = slice(hd * HEAD_D, (hd + 1) * HEAD_D)
        qf = qk[:, lanes]
        kf = qk[:, W + hd * HEAD_D:W + (hd + 1) * HEAD_D] * (HEAD_D ** -0.5)
        q = qf.astype(BF16)
        v = vv[:, lanes].astype(BF16)
        i_col = gates[:, hd:hd + 1]
        i_row = gates_t[hd:hd + 1, :]
        b_col = b_cols[:, N_HEADS + hd:N_HEADS + hd + 1]
        b_row = b_rows[N_HEADS + hd:N_HEADS + hd + 1, :]
        b_last = jnp.where(fwd, b_row[:, CHUNK - 1:CHUNK], b_row[:, 0:1])

        c_st = c_ref[0, 0, hd]
        n_st = n_ref[0, 0, hd]
        m_st = m_ref[0, 0, hd][:, 0:1]

        dmat = jnp.where(mask, b_col - b_row + i_row, -jnp.inf)
        inter = b_col + m_st
        m_t = jnp.maximum(inter, jnp.max(dmat, axis=-1, keepdims=True))
        s = _dot_nt(q, kf.astype(BF16)) * jnp.exp(dmat - m_t)
        a = jnp.exp(inter - m_t)
        num = _dot(s.astype(BF16), v) + a * _dot(q, c_st.astype(BF16))
        den = jnp.sum(s, axis=-1, keepdims=True) + a * jnp.sum(qf * n_st, axis=-1, keepdims=True)
        h_ref[0, 0, :, lanes] = num / jnp.maximum(jnp.abs(den), jnp.exp(-m_t))

        g_row = b_last - b_row + i_row
        g_col = b_last - b_col + i_col
        m_new = jnp.maximum(b_last + m_st, jnp.max(g_row, axis=-1, keepdims=True))
        kw = kf * jnp.exp(g_col - m_new)
        decay = jnp.exp(b_last + m_st - m_new)
        c_ref[0, 0, hd] = decay * c_st + _dot(jnp.transpose(kw).astype(BF16), v)
        n_ref[0, 0, hd] = decay * n_st + jnp.sum(kw, axis=0, keepdims=True)
        m_ref[0, 0, hd] = jnp.broadcast_to(m_new, (1, HEAD_D))


def _mlstm_scan(qkvo, halo, gates, gates_t, conv_w, c0, n0, m0):
    B, T, _ = qkvo.shape
    nc = T // CHUNK
    W = N_HEADS * HEAD_D

    def chunk(d, c):
        return c + d * (nc - 1 - 2 * c)

    st = lambda d, b, c: (d, b, 0, 0, 0)
    return pl.pallas_call(
        _mlstm_kernel,
        grid=(2, B, nc),
        in_specs=[pl.BlockSpec((1, CHUNK, W), lambda d, b, c: (b, chunk(d, c), 0)),
                  pl.BlockSpec((1, CHUNK, W), lambda d, b, c: (b, chunk(d, c), 1)),
                  pl.BlockSpec((1, CHUNK, W), lambda d, b, c: (b, chunk(d, c), 2)),
                  pl.BlockSpec((1, 1, 2, 2 * W), lambda d, b, c: (b, chunk(d, c), 0, 0)),
                  pl.BlockSpec((1, 1, CHUNK, 128), lambda d, b, c: (d, b, chunk(d, c), 0)),
                  pl.BlockSpec((1, 1, 2 * N_HEADS, CHUNK), lambda d, b, c: (d, b, 0, chunk(d, c))),
                  pl.BlockSpec((3, 2 * W), lambda d, b, c: (0, 0)),
                  pl.BlockSpec((1, 1, N_HEADS, HEAD_D, HEAD_D), st),
                  pl.BlockSpec((1, 1, N_HEADS, 1, HEAD_D), st),
                  pl.BlockSpec((1, 1, N_HEADS, 1, HEAD_D), st)],
        out_specs=[pl.BlockSpec((1, 1, CHUNK, W), lambda d, b, c: (d, b, chunk(d, c), 0)),
                   pl.BlockSpec((1, 1, N_HEADS, HEAD_D, HEAD_D), st),
                   pl.BlockSpec((1, 1, N_HEADS, 1, HEAD_D), st),
                   pl.BlockSpec((1, 1, N_HEADS, 1, HEAD_D), st)],
        out_shape=[jax.ShapeDtypeStruct((2, B, T, W), F32),
                   jax.ShapeDtypeStruct((2, B, N_HEADS, HEAD_D, HEAD_D), F32),
                   jax.ShapeDtypeStruct((2, B, N_HEADS, 1, HEAD_D), F32),
                   jax.ShapeDtypeStruct((2, B, N_HEADS, 1, HEAD_D), F32)],
        compiler_params=_params("parallel", "parallel", "arbitrary"),
        name="mlstm_scan",
    )(qkvo, qkvo, qkvo, halo, gates, gates_t, conv_w, c0, n0, m0)


def _dft_mats(n):
    j = np.arange(n, dtype=np.float64)
    ang = 2.0 * np.pi * np.outer(j, j) / n
    return np.cos(ang) / np.sqrt(n), np.sin(ang) / np.sqrt(n)


def _fft_stage1_kernel(a_ref, b_ref, w_ref, o_ref):
    ab = jnp.concatenate([a_ref[0], b_ref[0]], axis=0).astype(BF16)
    o_ref[0] = _dot(w_ref[...], ab)


def _fft_stage2_kernel(x_ref, tc_ref, ts_ref, w_ref, o_ref):
    nk = x_ref.shape[2]
    for j in range(nk):
        xr = x_ref[0, 0, j]
        xi = x_ref[0, 1, j]
        cw = jnp.concatenate([tc_ref[j]] * N_GROUPS, axis=1)
        sw = jnp.concatenate([ts_ref[j]] * N_GROUPS, axis=1)
        yr = xr * cw + xi * sw
        yi = xi * cw - xr * sw
        y = jnp.concatenate([yr, yi], axis=0).astype(BF16)
        o_ref[0, :, j, :] = _dot(w_ref[...], y)


def _fnet_seq_dft_long(a, b):
    B, T, W = a.shape
    T2 = CHUNK
    T1 = T // T2
    c1, s1 = _dft_mats(T1)
    c2, s2 = _dft_mats(T2)
    w1 = jnp.asarray(np.block([[c1, -s1], [-s1, -c1]]), BF16)
    w2 = jnp.asarray(np.concatenate([c2, s2], axis=1), BF16)
    ang = (2.0 * np.pi / T) * np.outer(np.arange(T1), np.arange(T2))
    tw_c = jnp.broadcast_to(jnp.asarray(np.cos(ang), F32)[:, :, None], (T1, T2, GROUP_W))
    tw_s = jnp.broadcast_to(jnp.asarray(np.sin(ang), F32)[:, :, None], (T1, T2, GROUP_W))

    ncol = T2 * W
    cb = ncol // 4
    x1 = pl.pallas_call(
        _fft_stage1_kernel,
        grid=(B, 4),
        in_specs=[pl.BlockSpec((1, T1, cb), lambda b_, j: (b_, 0, j)),
                  pl.BlockSpec((1, T1, cb), lambda b_, j: (b_, 0, j)),
                  pl.BlockSpec((2 * T1, 2 * T1), lambda b_, j: (0, 0))],
        out_specs=pl.BlockSpec((1, 2 * T1, cb), lambda b_, j: (b_, 0, j)),
        out_shape=jax.ShapeDtypeStruct((B, 2 * T1, ncol), F32),
        compiler_params=_params("parallel", "parallel"),
        name="fnet_dft_stage1",
    )(a.reshape(B, T1, ncol), b.reshape(B, T1, ncol), w1)

    nk = min(8, T1)
    y = pl.pallas_call(
        _fft_stage2_kernel,
        grid=(B, T1 // nk),
        in_specs=[pl.BlockSpec((1, 2, nk, T2, W), lambda b_, j: (b_, 0, j, 0, 0)),
                  pl.BlockSpec((nk, T2, GROUP_W), lambda b_, j: (j, 0, 0)),
                  pl.BlockSpec((nk, T2, GROUP_W), lambda b_, j: (j, 0, 0)),
                  pl.BlockSpec((T2, 2 * T2), lambda b_, j: (0, 0))],
        out_specs=pl.BlockSpec((1, T2, nk, W), lambda b_, j: (b_, 0, j, 0)),
        out_shape=jax.ShapeDtypeStruct((B, T2, T1, W), F32),
        compiler_params=_params("parallel", "parallel"),
        name="fnet_dft_stage2",
    )(x1.reshape(B, 2, T1, T2, W), tw_c, tw_s, w2)
    return y.reshape(B, T, W)


def _dft_short_kernel(a_ref, b_ref, w_ref, o_ref):
    ab = jnp.concatenate([a_ref[0], b_ref[0]], axis=0).astype(BF16)
    o_ref[0] = _dot(w_ref[...], ab)


def _fnet_seq_dft_short(a, b):
    B, T, W = a.shape
    c, s = _dft_mats(T)
    w = jnp.asarray(np.concatenate([c, -s], axis=1), BF16)
    return pl.pallas_call(
        _dft_short_kernel,
        grid=(B,),
        in_specs=[pl.BlockSpec((1, T, W), lambda b_: (b_, 0, 0)),
                  pl.BlockSpec((1, T, W), lambda b_: (b_, 0, 0)),
                  pl.BlockSpec((T, 2 * T), lambda b_: (0, 0))],
        out_specs=pl.BlockSpec((1, T, W), lambda b_: (b_, 0, 0)),
        out_shape=jax.ShapeDtypeStruct((B, T, W), F32),
        compiler_params=_params("parallel"),
        name="fnet_dft_short",
    )(a, b, w)


def _merge_kernel(x_ref, sh_ref, sc_ref, gt_ref, g_ref, hf_ref, hb_ref, o_ref, mg_ref, sgu_ref, fn_ref,
                  wmg_ref, bmg_ref, wbr_ref, wout_ref, out_ref):
    x = x_ref[0]
    h = _norm_mod(x, g_ref[...], sh_ref[0], sc_ref[0]).astype(BF16)
    hs = hf_ref[0, 0] + hb_ref[0, 0]
    parts = []
    for hd in range(N_HEADS):
        z = hs[:, hd * HEAD_D:(hd + 1) * HEAD_D]
        mu = jnp.mean(z, axis=-1, keepdims=True)
        var = jnp.mean(jnp.square(z - mu), axis=-1, keepdims=True)
        parts.append((z - mu) * lax.rsqrt(var + EPS))
    hn = jnp.concatenate(parts, axis=1) * mg_ref[...]
    ym = jax.nn.sigmoid(o_ref[0]) * hn
    y = None
    for r, br in enumerate((ym, sgu_ref[0], fn_ref[0])):
        cols = slice(r * D_MODEL, (r + 1) * D_MODEL)
        gate = jax.nn.sigmoid(_dot(h, wmg_ref[:, cols]) + bmg_ref[:, cols])
        term = gate * _dot(br.astype(BF16), wbr_ref[r])
        y = term if y is None else y + term
    out_ref[0] = x + gt_ref[0] * _dot(y.astype(BF16), wout_ref[...])


def _merge(x, shift, scale, gate, g, hdir, qkvo, mnorm_g, sgu, fnet, wmg, bmg, wbr, wout):
    B, T, D = x.shape
    tm = min(256, T)
    tok = lambda b_, i: (b_, i, 0)
    per_b = lambda b_, i: (b_, 0, 0)
    c2 = lambda b_, i: (0, 0)
    return pl.pallas_call(
        _merge_kernel,
        grid=(B, T // tm),
        in_specs=[pl.BlockSpec((1, tm, D), tok),
                  pl.BlockSpec((1, 1, D), per_b), pl.BlockSpec((1, 1, D), per_b), pl.BlockSpec((1, 1, D), per_b),
                  pl.BlockSpec((1, D), c2),
                  pl.BlockSpec((1, 1, tm, MIX_W), lambda b_, i: (0, b_, i, 0)),
                  pl.BlockSpec((1, 1, tm, MIX_W), lambda b_, i: (1, b_, i, 0)),
                  pl.BlockSpec((1, tm, MIX_W), lambda b_, i: (b_, i, 3)),
                  pl.BlockSpec((1, MIX_W), c2),
                  pl.BlockSpec((1, tm, MIX_W), tok), pl.BlockSpec((1, tm, MIX_W), tok),
                  pl.BlockSpec((D, 3 * D), c2), pl.BlockSpec((1, 3 * D), c2),
                  pl.BlockSpec((3, MIX_W, D), lambda b_, i: (0, 0, 0)),
                  pl.BlockSpec((D, D), c2)],
        out_specs=pl.BlockSpec((1, tm, D), tok),
        out_shape=jax.ShapeDtypeStruct(x.shape, F32),
        compiler_params=_params("parallel", "parallel"),
        name="branch_merge",
    )(x, shift, scale, gate, g, hdir, hdir, qkvo, mnorm_g, sgu, fnet, wmg, bmg, wbr, wout)


def _top_rows(s, k):
    n = s.shape[1]
    row = lax.broadcasted_iota(jnp.int32, s.shape, 0).astype(F32)
    slot = lax.broadcasted_iota(jnp.int32, (k, n), 0)
    vals = jnp.zeros((k, n), F32)
    ids = jnp.zeros((k, n), F32)
    for i in range(k):
        m = jnp.max(s, axis=0, keepdims=True)
        idx = jnp.min(jnp.where(s == m, row, float(s.shape[0])), axis=0, keepdims=True)
        vals = jnp.where(slot == i, m, vals)
        ids = jnp.where(slot == i, idx, ids)
        s = jnp.where(row == idx, -jnp.inf, s)
    return vals, ids


def _peer_topk_kernel(x_ref, sh_ref, sc_ref, g_ref, wq_ref, keys_ref, h2_ref, idx_ref, w_ref,
                      q_scr, idx_scr, w_scr):
    h2 = _norm_mod(x_ref[0], g_ref[...], sh_ref[0], sc_ref[0])
    h2_ref[0] = h2
    q_scr[...] = _dot(h2.astype(BF16), wq_ref[...])
    tm = h2.shape[0]
    n_half = TOPK // 2
    n_cand = TOPK + (TOPK - 1) * n_half
    n_exp = N_KEYS * N_KEYS
    heads_per_iter = 2

    def one_head(hd):
        tops = []
        for p in range(2):
            off = pl.multiple_of(hd * (2 * N_KEYS) + p * N_KEYS, N_KEYS)
            qhp = q_scr[:, pl.ds(off, N_KEYS)].astype(BF16)
            tops.append(_top_rows(_dot_nt(keys_ref[p], qhp), TOPK))
        (va, ia), (vb, ib) = tops
        cs = [va[0:1] + vb]
        ce = [ia[0:1] * N_KEYS + ib]
        for i in range(1, TOPK):
            cs.append(va[i:i + 1] + vb[0:n_half])
            ce.append(ia[i:i + 1] * N_KEYS + ib[0:n_half])
        cs = jnp.concatenate(cs, axis=0)
        row = lax.broadcasted_iota(jnp.int32, (n_cand, tm), 0).astype(F32)
        code = row * float(n_exp) + jnp.concatenate(ce, axis=0)
        slot = lax.broadcasted_iota(jnp.int32, (TOPK, tm), 0)
        best = jnp.zeros((TOPK, tm), F32)
        sel = jnp.zeros((TOPK, tm), F32)
        for i in range(TOPK):
            m = jnp.max(cs, axis=0, keepdims=True)
            first = jnp.min(jnp.where(cs == m, code, float(n_cand * n_exp)), axis=0, keepdims=True)
            best = jnp.where(slot == i, m, best)
            sel = jnp.where(slot == i, first, sel)
            cs = jnp.where(code == first, -jnp.inf, cs)
        sel = sel - float(n_exp) * jnp.floor(sel * (1.0 / n_exp))
        ex = jnp.exp(best - best[0:1])
        wts = ex / jnp.sum(ex, axis=0, keepdims=True)
        r0 = pl.multiple_of(hd * TOPK, TOPK)
        idx_scr[pl.ds(r0, TOPK), :] = sel
        w_scr[pl.ds(r0, TOPK), :] = wts

    def heads(it, carry):
        for k in range(heads_per_iter):
            one_head(it * heads_per_iter + k)
        return carry

    lax.fori_loop(0, PEER_HEADS // heads_per_iter, heads, 0)
    idx_ref[0] = (idx_scr[...].T).astype(jnp.int32) * SLAB
    w_ref[0] = w_scr[...].T


def _peer_topk(x, shift, scale, g, wq, keys):
    B, T, D = x.shape
    tm = min(256, T)
    tok = lambda b_, i: (b_, i, 0)
    per_b = lambda b_, i: (b_, 0, 0)
    c2 = lambda b_, i: (0, 0)
    WQ = wq.shape[1]
    return pl.pallas_call(
        _peer_topk_kernel,
        grid=(B, T // tm),
        in_specs=[pl.BlockSpec((1, tm, D), tok),
                  pl.BlockSpec((1, 1, D), per_b), pl.BlockSpec((1, 1, D), per_b),
                  pl.BlockSpec((1, D), c2),
                  pl.BlockSpec((D, WQ), c2),
                  pl.BlockSpec((2, N_KEYS, N_KEYS), lambda b_, i: (0, 0, 0))],
        out_specs=[pl.BlockSpec((1, tm, D), tok), pl.BlockSpec((1, tm, N_SEL), tok),
                   pl.BlockSpec((1, tm, N_SEL), tok)],
        out_shape=[jax.ShapeDtypeStruct((B, T, D), F32),
                   jax.ShapeDtypeStruct((B, T, N_SEL), jnp.int32),
                   jax.ShapeDtypeStruct((B, T, N_SEL), F32)],
        scratch_shapes=[pltpu.VMEM((tm, WQ), F32), pltpu.VMEM((N_SEL, tm), F32), pltpu.VMEM((N_SEL, tm), F32)],
        compiler_params=_params("parallel", "parallel"),
        name="peer_topk",
    )(x, shift, scale, g, wq, keys)


def _pack_table(tab):
    bits = lax.bitcast_convert_type(tab.astype(TABLE_DT), jnp.uint16).astype(jnp.uint32)
    word = bits[:, :HALF_D] | (bits[:, HALF_D:] << 16)
    return word.reshape(tab.shape[0] * SLAB, 128)


def _gather_rows(sidx_ref, slot, j, tab_ref, g_ref):
    for m in range(N_SEL):
        e = pl.multiple_of(sidx_ref[slot, j, m], SLAB)
        g_ref[pl.ds(m * SLAB, SLAB), :] = tab_ref[pl.ds(e, SLAB), :]
    chunks = [g_ref[pl.ds(c, N_SEL, stride=SLAB), :] for c in range(SLAB)]
    return pltpu.bitcast(jnp.concatenate(chunks, axis=1), TABLE_DT)


def _two_rows(r0, r1, width):
    sub = lax.broadcasted_iota(jnp.int32, (8, width), 0)
    return jnp.where(sub == 0, r0, jnp.where(sub == 1, r1, 0.0))


def _idx_copy(idx_hbm, sidx_ref, sem, row0, slot):
    return pltpu.make_async_copy(idx_hbm.at[pl.ds(row0, STAGE), :], sidx_ref.at[slot], sem.at[slot])


def _for_token_groups(idx_hbm, sidx_ref, sem, tm, process):
    i = pl.program_id(0)
    total = pl.num_programs(0) * tm
    tok0 = i * tm

    @pl.when(i == 0)
    def _():
        _idx_copy(idx_hbm, sidx_ref, sem, 0, 0).start()

    def stage(slot, base):
        for q in range(STAGE // GROUP):
            process(slot, q * GROUP, base + q * GROUP)

    def pair(k, carry):
        base = pl.multiple_of(k * (2 * STAGE), 2 * STAGE)
        _idx_copy(idx_hbm, sidx_ref, sem, 0, 0).wait()
        _idx_copy(idx_hbm, sidx_ref, sem, tok0 + base + STAGE, 1).start()
        stage(0, base)
        _idx_copy(idx_hbm, sidx_ref, sem, 0, 1).wait()
        nxt = tok0 + base + 2 * STAGE

        @pl.when(nxt < total)
        def _():
            _idx_copy(idx_hbm, sidx_ref, sem, nxt, 0).start()

        stage(1, base + STAGE)
        return carry

    lax.fori_loop(0, tm // (2 * STAGE), pair, 0)


def _peer_u_kernel(idx_hbm, h2_ref, w_ref, ex_ref, tab_ref, a_ref, g0_ref, g1_ref, act_ref, sidx_ref, sem):
    tm = h2_ref.shape[0]
    lane = lax.broadcasted_iota(jnp.int32, (8, 2 * N_SEL), 1)
    even = (lane & 1) == 0
    sub = lax.broadcasted_iota(jnp.int32, (8, 2 * N_SEL), 0)

    def process(slot, j0, base):
        xg = h2_ref[pl.ds(base, GROUP), :]
        rows = jnp.zeros((GROUP, 2 * N_SEL), F32)
        for j in range(GROUP):
            g = _gather_rows(sidx_ref, slot, j0 + j, tab_ref, g0_ref if j % 2 == 0 else g1_ref)
            lhs = _two_rows(xg[j:j + 1, :HALF_D], xg[j:j + 1, HALF_D:], HALF_D).astype(TABLE_DT)
            out = _dot_nt(lhs, g)
            z = jnp.where(even, out, pltpu.roll(out, 7, axis=0))
            s = z + pltpu.roll(z, 2 * N_SEL - 1, axis=1)
            rows = jnp.where(sub == j, s[0:1], rows)
        act_ref[pl.ds(base, GROUP), :] = rows

    _for_token_groups(idx_hbm, sidx_ref, sem, tm, process)
    w2 = _dot_exact(w_ref[...], ex_ref[...])
    a_ref[...] = jax.nn.gelu(act_ref[...]) * w2


def _peer_v_kernel(idx_hbm, a_ref, x_ref, gt_ref, tab_ref, o_ref, g0_ref, g1_ref, sidx_ref, sem):
    tm = a_ref.shape[0]
    sub = lax.broadcasted_iota(jnp.int32, (8, D_MODEL), 0)

    def process(slot, j0, base):
        ag = a_ref[pl.ds(base, GROUP), :]
        rows = jnp.zeros((GROUP, D_MODEL), F32)
        for j in range(GROUP):
            g = _gather_rows(sidx_ref, slot, j0 + j, tab_ref, g0_ref if j % 2 == 0 else g1_ref)
            a0 = jnp.broadcast_to(ag[j:j + 1, :], (8, 2 * N_SEL))
            lhs = _two_rows(a0, pltpu.roll(a0, 1, axis=1), 2 * N_SEL).astype(TABLE_DT)
            out = _dot(lhs, g)
            rows = jnp.where(sub == j, jnp.concatenate([out[0:1], out[1:2]], axis=1), rows)
        o_ref[pl.ds(base, GROUP), :] = x_ref[pl.ds(base, GROUP), :] + gt_ref[0] * rows

    _for_token_groups(idx_hbm, sidx_ref, sem, tm, process)


def _peer_expert_mix(x, gate, h2, idx, w, utab, vtab):
    B, T, D = x.shape
    N = B * T
    tm = 128
    tpb = T // tm
    expand = np.zeros((N_SEL, 2 * N_SEL), np.float32)
    expand[np.arange(N_SEL), 2 * np.arange(N_SEL)] = 1.0
    flat = lambda i: (i, 0)
    idx2 = idx.reshape(N, N_SEL)
    gscr = pltpu.VMEM((N_SEL * SLAB, 128), jnp.uint32)
    stage = [pltpu.SMEM((2, STAGE, N_SEL), jnp.int32), pltpu.SemaphoreType.DMA((2,))]
    a = pl.pallas_call(
        _peer_u_kernel,
        grid=(N // tm,),
        in_specs=[pl.BlockSpec(memory_space=pl.ANY),
                  pl.BlockSpec((tm, D), flat),
                  pl.BlockSpec((tm, N_SEL), flat),
                  pl.BlockSpec((N_SEL, 2 * N_SEL), lambda i: (0, 0)),
                  pl.BlockSpec(memory_space=pltpu.VMEM)],
        out_specs=pl.BlockSpec((tm, 2 * N_SEL), flat),
        out_shape=jax.ShapeDtypeStruct((N, 2 * N_SEL), F32),
        scratch_shapes=[gscr, gscr, pltpu.VMEM((tm, 2 * N_SEL), F32)] + stage,
        compiler_params=_params("arbitrary"),
        name="peer_expert_in",
    )(idx2, h2.reshape(N, D), w.reshape(N, N_SEL), jnp.asarray(expand), utab)
    out = pl.pallas_call(
        _peer_v_kernel,
        grid=(N // tm,),
        in_specs=[pl.BlockSpec(memory_space=pl.ANY),
                  pl.BlockSpec((tm, 2 * N_SEL), flat),
                  pl.BlockSpec((tm, D), flat),
                  pl.BlockSpec((1, 1, D), lambda i: (i // tpb, 0, 0)),
                  pl.BlockSpec(memory_space=pltpu.VMEM)],
        out_specs=pl.BlockSpec((tm, D), flat),
        out_shape=jax.ShapeDtypeStruct((N, D), F32),
        scratch_shapes=[gscr, gscr] + stage,
        compiler_params=_params("arbitrary"),
        name="peer_expert_out",
    )(idx2, a, x.reshape(N, D), gate, vtab)
    return out.reshape(B, T, D)


def _halo_rows(qk):
    B, T, C = qk.shape
    z = jnp.zeros((B, 1, C), qk.dtype)
    prev = jnp.concatenate([z, qk[:, CHUNK - 1:T - 1:CHUNK]], axis=1)
    nxt = jnp.concatenate([qk[:, CHUNK::CHUNK], z], axis=1)
    return jnp.stack([prev, nxt], axis=2)


def _stream_layer(x, mod, lw, init):
    B, T, D = x.shape
    sh1, sc1, gt1, sh2, sc2, gt2 = mod
    qkvo, gates = _mlstm_proj(x, sh1, sc1, lw["g1"], lw["w_qkvo"], lw["b_qkvo"], lw["w_gate"], lw["b_gate"])
    sgu, fa, fb = _sgu_fnet_proj(x, sh1, sc1, lw["g1"], lw["w_sf"], lw["b_sf"], lw["sgu_w"], lw["sgu_bt"],
                                 lw["dft_c"], lw["dft_s"])
    halo = _halo_rows(qkvo[:, :, :2 * MIX_W])
    gd = jnp.transpose(gates[:, :, :4 * N_HEADS].reshape(B, T, 2, 2 * N_HEADS), (2, 0, 1, 3))
    gd_pad = jnp.pad(gd, ((0, 0), (0, 0), (0, 0), (0, 128 - 2 * N_HEADS)))
    hdir, c_fin, n_fin, m_fin = _mlstm_scan(qkvo, halo, gd_pad, jnp.swapaxes(gd, 2, 3), lw["conv"], *init)
    fnet = _fnet_seq_dft_long(fa, fb) if T > 2 * CHUNK else _fnet_seq_dft_short(fa, fb)
    x = _merge(x, sh1, sc1, gt1, lw["g1"], hdir, qkvo, lw["mnorm_g"], sgu, fnet,
               lw["w_mg"], lw["b_mg"], lw["w_br"], lw["w_out"])
    h2, idx, w = _peer_topk(x, sh2, sc2, lw["g2"], lw["wq"], lw["keys"])
    x = _peer_expert_mix(x, gt2, h2, idx, w, lw["utab"], lw["vtab"])
    return x, (c_fin, n_fin, m_fin)


def _pos_embed(rows):
    quarter = D_MODEL // 4
    omega = 1.0 / (10000.0 ** (jnp.arange(quarter, dtype=F32) / quarter))
    r = jnp.repeat(jnp.arange(rows, dtype=F32), GRID_W)[:, None] * omega
    cc = jnp.tile(jnp.arange(GRID_W, dtype=F32), rows)[:, None] * omega
    return jnp.concatenate([jnp.sin(r), jnp.cos(r), jnp.sin(cc), jnp.cos(cc)], axis=-1)


def kernel(x, c, ctx, c_ctx, w_mod, b_mod, norm1_g, norm2_g, w_in, b_in, conv_qk, mlstm_norm_g, sgu_w, sgu_b, w_br, w_out, peer_wq, peer_keys, peer_u, peer_v, final_g):
    B, T, D = x.shape
    depth = w_mod.shape[0]
    x = _add_pos(x, _pos_embed(T // GRID_W))

    cc = jnp.zeros((8, D), F32).at[:B].set(c).at[B].set(c_ctx)
    mods = _modulation(cc, w_mod, b_mod)

    cdft, sdft = _dft_mats(GROUP_W)
    dft_c = jnp.asarray(cdft, BF16)
    dft_s = jnp.asarray(sdft, BF16)
    o_q, o_g = 4 * MIX_W, 4 * MIX_W + 4 * N_HEADS
    o_sf, o_mg = o_g, o_g + 3 * MIX_W

    def layer(carry, lp):
        x, ctx = carry
        (mod, g1, g2, win, bin_, conv, mng, sw, sb, wbr, wout, wq, keys, pu, pv) = lp
        lw = {
            "g1": g1.reshape(1, D), "g2": g2.reshape(1, D),
            "w_qkvo": win[:, :o_q].astype(BF16), "b_qkvo": bin_[:o_q].reshape(1, -1),
            "w_gate": jnp.pad(win[:, o_q:o_g], ((0, 0), (0, 128 - 4 * N_HEADS))),
            "b_gate": jnp.pad(bin_[o_q:o_g], (0, 128 - 4 * N_HEADS)).reshape(1, -1),
            "w_sf": win[:, o_sf:o_mg].astype(BF16), "b_sf": bin_[o_sf:o_mg].reshape(1, -1),
            "w_mg": win[:, o_mg:].astype(BF16), "b_mg": bin_[o_mg:].reshape(1, -1),
            "conv": conv, "mnorm_g": mng.reshape(1, -1),
            "sgu_w": sw.astype(BF16), "sgu_bt": sb.T,
            "dft_c": dft_c, "dft_s": dft_s,
            "w_br": wbr.astype(BF16), "w_out": wout.astype(BF16),
            "wq": wq.astype(BF16), "keys": keys.astype(BF16),
            "utab": _pack_table(pu), "vtab": _pack_table(pv),
        }
        mod_x = [mod[:B, i * D:(i + 1) * D].reshape(B, 1, D) for i in range(N_MOD)]
        mod_c = [jnp.broadcast_to(mod[B, i * D:(i + 1) * D].reshape(1, 1, D), (B, 1, D)) for i in range(N_MOD)]
        zero = (jnp.zeros((2, B, N_HEADS, HEAD_D, HEAD_D), F32),
                jnp.zeros((2, B, N_HEADS, 1, HEAD_D), F32),
                jnp.zeros((2, B, N_HEADS, 1, HEAD_D), F32))
        ctx_new, ctx_state = _stream_layer(ctx, mod_c, lw, zero)
        x_new, _ = _stream_layer(x, mod_x, lw, ctx_state)
        return (x_new, ctx_new), None

    xs = (mods, norm1_g, norm2_g, w_in, b_in, conv_qk, mlstm_norm_g, sgu_w, sgu_b, w_br, w_out,
          peer_wq, peer_keys, peer_u, peer_v)
    (x, ctx), _ = lax.scan(layer, (x, ctx), xs)
    return _final_norm(x, final_g)
```

```python
import numpy as np
import jax
import jax.numpy as jnp
from jax import lax
from jax.experimental import pallas as pl
from jax.experimental.pallas import tpu as pltpu

F32 = jnp.float32
BF16 = jnp.bfloat16
TABLE_DT = jnp.bfloat16
HIGHEST = lax.Precision.HIGHEST

D_MODEL = 1024
GRID_W = 64
EPS = 1e-6
N_MOD = 6
N_HEADS = 4
HEAD_D = 128
CHUNK = 128
MIX_W = 512
N_GROUPS = 4
GROUP_W = 128
PEER_HEADS = 8
N_KEYS = 128
TOPK = 16
N_SEL = PEER_HEADS * TOPK
HALF_D = D_MODEL // 2
SLAB = 4
GROUP = 8
STAGE = 64
VMEM_LIMIT = 56 * 1024 * 1024


def _params(*sem):
    return pltpu.CompilerParams(dimension_semantics=sem, vmem_limit_bytes=VMEM_LIMIT)


def _dot(a, b):
    return jnp.dot(a, b, preferred_element_type=F32)


def _dot_nt(a, b):
    return lax.dot_general(a, b, (((1,), (1,)), ((), ())), preferred_element_type=F32)


def _dot_tn(a, b):
    return lax.dot_general(a, b, (((0,), (0,)), ((), ())), preferred_element_type=F32)


def _dot_exact(a, b):
    return jnp.dot(a, b, preferred_element_type=F32, precision=HIGHEST)


def _norm_mod(x, g, shift, scale):
    y = x * lax.rsqrt(jnp.mean(x * x, axis=-1, keepdims=True) + EPS)
    return (y * g) * (1.0 + scale) + shift


def _add_pos_kernel(x_ref, p_ref, o_ref):
    o_ref[0] = x_ref[0] + p_ref[...]


def _add_pos(x, pos):
    B, T, D = x.shape
    tm = 512
    return pl.pallas_call(
        _add_pos_kernel,
        grid=(B, T // tm),
        in_specs=[pl.BlockSpec((1, tm, D), lambda b, i: (b, i, 0)),
                  pl.BlockSpec((tm, D), lambda b, i: (i, 0))],
        out_specs=pl.BlockSpec((1, tm, D), lambda b, i: (b, i, 0)),
        out_shape=jax.ShapeDtypeStruct(x.shape, x.dtype),
        compiler_params=_params("parallel", "parallel"),
        name="add_pos",
    )(x, pos)


def _final_norm_kernel(x_ref, g_ref, o_ref):
    x = x_ref[0]
    o_ref[0] = x * lax.rsqrt(jnp.mean(x * x, axis=-1, keepdims=True) + EPS) * g_ref[...]


def _final_norm(x, g):
    B, T, D = x.shape
    tm = 512
    return pl.pallas_call(
        _final_norm_kernel,
        grid=(B, T // tm),
        in_specs=[pl.BlockSpec((1, tm, D), lambda b, i: (b, i, 0)),
                  pl.BlockSpec((1, D), lambda b, i: (0, 0))],
        out_specs=pl.BlockSpec((1, tm, D), lambda b, i: (b, i, 0)),
        out_shape=jax.ShapeDtypeStruct(x.shape, x.dtype),
        compiler_params=_params("parallel", "parallel"),
        name="final_norm",
    )(x, g.reshape(1, D))


def _mod_kernel(c_ref, w_ref, b_ref, o_ref):
    c = c_ref[...]
    s = c * jax.nn.sigmoid(c)
    o_ref[0] = _dot_exact(s, w_ref[0]) + b_ref[0]


def _modulation(cc, w_mod, b_mod):
    depth, D, W = w_mod.shape
    return pl.pallas_call(
        _mod_kernel,
        grid=(depth, W // D),
        in_specs=[pl.BlockSpec((8, D), lambda l, j: (0, 0)),
                  pl.BlockSpec((1, D, D), lambda l, j: (l, 0, j)),
                  pl.BlockSpec((1, 1, D), lambda l, j: (l, 0, j))],
        out_specs=pl.BlockSpec((1, 8, D), lambda l, j: (l, 0, j)),
        out_shape=jax.ShapeDtypeStruct((depth, 8, W), F32),
        compiler_params=_params("parallel", "parallel"),
        name="adaln_mod",
    )(cc, w_mod, b_mod.reshape(depth, 1, W))


def _mproj_kernel(x_ref, sh_ref, sc_ref, g_ref, w_ref, b_ref, wg_ref, bg_ref, o_ref, og_ref, ogt_ref):
    h = _norm_mod(x_ref[0], g_ref[...], sh_ref[0], sc_ref[0])
    o_ref[0] = _dot(h.astype(BF16), w_ref[...]) + b_ref[...]
    gates = _dot_exact(h, wg_ref[...]) + bg_ref[...]
    for d in range(2):
        gd = gates[:, d * 128:(d + 1) * 128]
        og_ref[d, 0] = gd
        ogt_ref[d, 0] = jnp.transpose(gd)[:2 * N_HEADS, :]


def _mlstm_proj(x, shift, scale, g, w, b, wg, bg):
    B, T, D = x.shape
    tm = min(512, T)
    W = w.shape[1]
    NG = wg.shape[1]
    tok = lambda b_, i: (b_, i, 0)
    per_b = lambda b_, i: (b_, 0, 0)
    const = lambda b_, i: (0, 0)
    return pl.pallas_call(
        _mproj_kernel,
        grid=(B, T // tm),
        in_specs=[pl.BlockSpec((1, tm, D), tok),
                  pl.BlockSpec((1, 1, D), per_b), pl.BlockSpec((1, 1, D), per_b),
                  pl.BlockSpec((1, D), const),
                  pl.BlockSpec((D, W), const), pl.BlockSpec((1, W), const),
                  pl.BlockSpec((D, NG), const), pl.BlockSpec((1, NG), const)],
        out_specs=[pl.BlockSpec((1, tm, W), tok),
                   pl.BlockSpec((2, 1, tm, 128), lambda b_, i: (0, b_, i, 0)),
                   pl.BlockSpec((2, 1, 2 * N_HEADS, tm), lambda b_, i: (0, b_, 0, i))],
        out_shape=[jax.ShapeDtypeStruct((B, T, W), F32),
                   jax.ShapeDtypeStruct((2, B, T, 128), F32),
                   jax.ShapeDtypeStruct((2, B, 2 * N_HEADS, T), F32)],
        compiler_params=_params("parallel", "parallel"),
        name="mlstm_proj",
    )(x, shift, scale, g, w, b, wg, bg)


def _sgu_fnet_kernel(x_ref, sh_ref, sc_ref, g_ref, w_ref, b_ref, sw_ref, sbt_ref, cc_ref, cs_ref,
                     sgu_ref, a_ref, bm_ref):
    h = _norm_mod(x_ref[0], g_ref[...], sh_ref[0], sc_ref[0])
    p = _dot(h.astype(BF16), w_ref[...]) + b_ref[...]
    tm = p.shape[0]
    u = jax.nn.gelu(p[:, :MIX_W])
    vf = jax.nn.gelu(p[:, MIX_W:2 * MIX_W])
    mu = jnp.mean(vf, axis=-1, keepdims=True)
    var = jnp.mean(jnp.square(vf - mu), axis=-1, keepdims=True)
    v = ((vf - mu) * lax.rsqrt(var + EPS)).astype(BF16)
    for j in range(tm // CHUNK):
        rows = slice(j * CHUNK, (j + 1) * CHUNK)
        for gi in range(N_GROUPS):
            cols = slice(gi * GROUP_W, (gi + 1) * GROUP_W)
            s = _dot(sw_ref[gi], v[rows, cols]) + sbt_ref[:, gi:gi + 1]
            sgu_ref[0, rows, cols] = u[rows, cols] * s
    fz = p[:, 2 * MIX_W:].astype(BF16)
    for gi in range(N_GROUPS):
        cols = slice(gi * GROUP_W, (gi + 1) * GROUP_W)
        a_ref[0, :, cols] = _dot(fz[:, cols], cc_ref[...])
        bm_ref[0, :, cols] = _dot(fz[:, cols], cs_ref[...])


def _sgu_fnet_proj(x, shift, scale, g, w, b, sw, sbt, cc, cs):
    B, T, D = x.shape
    tm = min(512, T)
    W = w.shape[1]
    tok = lambda b_, i: (b_, i, 0)
    per_b = lambda b_, i: (b_, 0, 0)
    c2 = lambda b_, i: (0, 0)
    c3 = lambda b_, i: (0, 0, 0)
    out = jax.ShapeDtypeStruct((B, T, MIX_W), F32)
    return pl.pallas_call(
        _sgu_fnet_kernel,
        grid=(B, T // tm),
        in_specs=[pl.BlockSpec((1, tm, D), tok),
                  pl.BlockSpec((1, 1, D), per_b), pl.BlockSpec((1, 1, D), per_b),
                  pl.BlockSpec((1, D), c2),
                  pl.BlockSpec((D, W), c2), pl.BlockSpec((1, W), c2),
                  pl.BlockSpec((N_GROUPS, CHUNK, CHUNK), c3), pl.BlockSpec((CHUNK, N_GROUPS), c2),
                  pl.BlockSpec((GROUP_W, GROUP_W), c2), pl.BlockSpec((GROUP_W, GROUP_W), c2)],
        out_specs=[pl.BlockSpec((1, tm, MIX_W), tok)] * 3,
        out_shape=[out, out, out],
        compiler_params=_params("parallel", "parallel"),
        name="sgu_fnet_proj",
    )(x, shift, scale, g, w, b, sw, sbt, cc, cs)


def _log_sigmoid(x):
    return jnp.minimum(x, 0.0) - jnp.log1p(jnp.exp(-jnp.abs(x)))


def _mlstm_kernel(q_ref, k_ref, v_ref, halo_ref, gt_ref, gtt_ref, cw_ref, c0_ref, n0_ref, m0_ref,
                  h_ref, c_ref, n_ref, m_ref):
    d = pl.program_id(0)
    c = pl.program_id(2)

    @pl.when(c == 0)
    def _():
        c_ref[0, 0] = c0_ref[0, 0]
        n_ref[0, 0] = n0_ref[0, 0]
        m_ref[0, 0] = m0_ref[0, 0]

    row = lax.broadcasted_iota(jnp.int32, (CHUNK, CHUNK), 0)
    col = lax.broadcasted_iota(jnp.int32, (CHUNK, CHUNK), 1)
    fwd = d == 0
    sgn = jnp.where(fwd, 1, -1)
    mask = (col - row) * sgn <= 0
    tri = mask.astype(F32)
    tri_t = ((row - col) * sgn <= 0).astype(F32)

    W = N_HEADS * HEAD_D
    ridx = lax.broadcasted_iota(jnp.int32, (CHUNK, 2 * W), 0)
    qk = jnp.concatenate([q_ref[0], k_ref[0]], axis=1)
    prev = jnp.where(ridx == 0, halo_ref[0, 0, 0:1, :], pltpu.roll(qk, 1, axis=0))
    nxt = jnp.where(ridx == CHUNK - 1, halo_ref[0, 0, 1:2, :], pltpu.roll(qk, CHUNK - 1, axis=0))
    qk = prev * cw_ref[0:1, :] + qk * cw_ref[1:2, :] + nxt * cw_ref[2:3, :]
    qk = qk * jax.nn.sigmoid(qk)

    gates = gt_ref[0, 0]
    gates_t = gtt_ref[0, 0]
    b_cols = _dot_exact(tri, _log_sigmoid(gates))
    b_rows = _dot_exact(_log_sigmoid(gates_t), tri_t)
    vv = v_ref[0]

    for hd in range(N_HEADS):
        lanes = slice(hd * HEAD_D, (hd + 1) * HEAD_D)
        qf = qk[:, lanes]
        kf = qk[:, W + hd * HEAD_D:W + (hd + 1) * HEAD_D] * (HEAD_D ** -0.5)
        q = qf.astype(BF16)
        v = vv[:, lanes].astype(BF16)
        i_col = gates[:, hd:hd + 1]
        i_row = gates_t[hd:hd + 1, :]
        b_col = b_cols[:, N_HEADS + hd:N_HEADS + hd + 1]
        b_row = b_rows[N_HEADS + hd:N_HEADS + hd + 1, :]
        b_last = jnp.where(fwd, b_row[:, CHUNK - 1:CHUNK], b_row[:, 0:1])

        c_st = c_ref[0, 0, hd]
        n_st = n_ref[0, 0, hd]
        m_st = m_ref[0, 0, hd][:, 0:1]

        dmat = jnp.where(mask, b_col - b_row + i_row, -jnp.inf)
        inter = b_col + m_st
        m_t = jnp.maximum(inter, jnp.max(dmat, axis=-1, keepdims=True))
        s = _dot_nt(q, kf.astype(BF16)) * jnp.exp(dmat - m_t)
        a = jnp.exp(inter - m_t)
        num = _dot(s.astype(BF16), v) + a * _dot(q, c_st.astype(BF16))
        den = jnp.sum(s, axis=-1, keepdims=True) + a * jnp.sum(qf * n_st, axis=-1, keepdims=True)
        h_ref[0, 0, :, lanes] = num / jnp.maximum(jnp.abs(den), jnp.exp(-m_t))

        g_row = b_last - b_row + i_row
        g_col = b_last - b_col + i_col
        m_new = jnp.maximum(b_last + m_st, jnp.max(g_row, axis=-1, keepdims=True))
        kw = kf * jnp.exp(g_col - m_new)
        decay = jnp.exp(b_last + m_st - m_new)
        c_ref[0, 0, hd] = decay * c_st + _dot(jnp.transpose(kw).astype(BF16), v)
        n_ref[0, 0, hd] = decay * n_st + jnp.sum(kw, axis=0, keepdims=True)
        m_ref[0, 0, hd] = jnp.broadcast_to(m_new, (1, HEAD_D))


def _mlstm_scan(qkvo, halo, gates, gates_t, conv_w, c0, n0, m0):
    B, T, _ = qkvo.shape
    nc = T // CHUNK
    W = N_HEADS * HEAD_D

    def chunk(d, c):
        return c + d * (nc - 1 - 2 * c)

    st = lambda d, b, c: (d, b, 0, 0, 0)
    return pl.pallas_call(
        _mlstm_kernel,
        grid=(2, B, nc),
        in_specs=[pl.BlockSpec((1, CHUNK, W), lambda d, b, c: (b, chunk(d, c), 0)),
                  pl.BlockSpec((1, CHUNK, W), lambda d, b, c: (b, chunk(d, c), 1)),
                  pl.BlockSpec((1, CHUNK, W), lambda d, b, c: (b, chunk(d, c), 2)),
                  pl.BlockSpec((1, 1, 2, 2 * W), lambda d, b, c: (b, chunk(d, c), 0, 0)),
                  pl.BlockSpec((1, 1, CHUNK, 128), lambda d, b, c: (d, b, chunk(d, c), 0)),
                  pl.BlockSpec((1, 1, 2 * N_HEADS, CHUNK), lambda d, b, c: (d, b, 0, chunk(d, c))),
                  pl.BlockSpec((3, 2 * W), lambda d, b, c: (0, 0)),
                  pl.BlockSpec((1, 1, N_HEADS, HEAD_D, HEAD_D), st),
                  pl.BlockSpec((1, 1, N_HEADS, 1, HEAD_D), st),
                  pl.BlockSpec((1, 1, N_HEADS, 1, HEAD_D), st)],
        out_specs=[pl.BlockSpec((1, 1, CHUNK, W), lambda d, b, c: (d, b, chunk(d, c), 0)),
                   pl.BlockSpec((1, 1, N_HEADS, HEAD_D, HEAD_D), st),
                   pl.BlockSpec((1, 1, N_HEADS, 1, HEAD_D), st),
                   pl.BlockSpec((1, 1, N_HEADS, 1, HEAD_D), st)],
        out_shape=[jax.ShapeDtypeStruct((2, B, T, W), F32),
                   jax.ShapeDtypeStruct((2, B, N_HEADS, HEAD_D, HEAD_D), F32),
                   jax.ShapeDtypeStruct((2, B, N_HEADS, 1, HEAD_D), F32),
                   jax.ShapeDtypeStruct((2, B, N_HEADS, 1, HEAD_D), F32)],
        compiler_params=_params("parallel", "parallel", "arbitrary"),
        name="mlstm_scan",
    )(qkvo, qkvo, qkvo, halo, gates, gates_t, conv_w, c0, n0, m0)


def _dft_mats(n):
    j = np.arange(n, dtype=np.float64)
    ang = 2.0 * np.pi * np.outer(j, j) / n
    return np.cos(ang) / np.sqrt(n), np.sin(ang) / np.sqrt(n)


def _fft_stage1_kernel(a_ref, b_ref, w_ref, o_ref):
    ab = jnp.concatenate([a_ref[0], b_ref[0]], axis=0).astype(BF16)
    o_ref[0] = _dot(w_ref[...], ab)


def _fft_stage2_kernel(x_ref, tc_ref, ts_ref, w_ref, o_ref):
    nk = x_ref.shape[2]
    for j in range(nk):
        xr = x_ref[0, 0, j]
        xi = x_ref[0, 1, j]
        cw = jnp.concatenate([tc_ref[j]] * N_GROUPS, axis=1)
        sw = jnp.concatenate([ts_ref[j]] * N_GROUPS, axis=1)
        yr = xr * cw + xi * sw
        yi = xi * cw - xr * sw
        y = jnp.concatenate([yr, yi], axis=0).astype(BF16)
        o_ref[0, :, j, :] = _dot(w_ref[...], y)


def _fnet_seq_dft_long(a, b):
    B, T, W = a.shape
    T2 = CHUNK
    T1 = T // T2
    c1, s1 = _dft_mats(T1)
    c2, s2 = _dft_mats(T2)
    w1 = jnp.asarray(np.block([[c1, -s1], [-s1, -c1]]), BF16)
    w2 = jnp.asarray(np.concatenate([c2, s2], axis=1), BF16)
    ang = (2.0 * np.pi / T) * np.outer(np.arange(T1), np.arange(T2))
    tw_c = jnp.broadcast_to(jnp.asarray(np.cos(ang), F32)[:, :, None], (T1, T2, GROUP_W))
    tw_s = jnp.broadcast_to(jnp.asarray(np.sin(ang), F32)[:, :, None], (T1, T2, GROUP_W))

    ncol = T2 * W
    cb = ncol // 4
    x1 = pl.pallas_call(
        _fft_stage1_kernel,
        grid=(B, 4),
        in_specs=[pl.BlockSpec((1, T1, cb), lambda b_, j: (b_, 0, j)),
                  pl.BlockSpec((1, T1, cb), lambda b_, j: (b_, 0, j)),
                  pl.BlockSpec((2 * T1, 2 * T1), lambda b_, j: (0, 0))],
        out_specs=pl.BlockSpec((1, 2 * T1, cb), lambda b_, j: (b_, 0, j)),
        out_shape=jax.ShapeDtypeStruct((B, 2 * T1, ncol), F32),
        compiler_params=_params("parallel", "parallel"),
        name="fnet_dft_stage1",
    )(a.reshape(B, T1, ncol), b.reshape(B, T1, ncol), w1)

    nk = min(8, T1)
    y = pl.pallas_call(
        _fft_stage2_kernel,
        grid=(B, T1 // nk),
        in_specs=[pl.BlockSpec((1, 2, nk, T2, W), lambda b_, j: (b_, 0, j, 0, 0)),
                  pl.BlockSpec((nk, T2, GROUP_W), lambda b_, j: (j, 0, 0)),
                  pl.BlockSpec((nk, T2, GROUP_W), lambda b_, j: (j, 0, 0)),
                  pl.BlockSpec((T2, 2 * T2), lambda b_, j: (0, 0))],
        out_specs=pl.BlockSpec((1, T2, nk, W), lambda b_, j: (b_, 0, j, 0)),
        out_shape=jax.ShapeDtypeStruct((B, T2, T1, W), F32),
        compiler_params=_params("parallel", "parallel"),
        name="fnet_dft_stage2",
    )(x1.reshape(B, 2, T1, T2, W), tw_c, tw_s, w2)
    return y.reshape(B, T, W)


def _dft_short_kernel(a_ref, b_ref, w_ref, o_ref):
    ab = jnp.concatenate([a_ref[0], b_ref[0]], axis=0).astype(BF16)
    o_ref[0] = _dot(w_ref[...], ab)


def _fnet_seq_dft_short(a, b):
    B, T, W = a.shape
    c, s = _dft_mats(T)
    w = jnp.asarray(np.concatenate([c, -s], axis=1), BF16)
    return pl.pallas_call(
        _dft_short_kernel,
        grid=(B,),
        in_specs=[pl.BlockSpec((1, T, W), lambda b_: (b_, 0, 0)),
                  pl.BlockSpec((1, T, W), lambda b_: (b_, 0, 0)),
                  pl.BlockSpec((T, 2 * T), lambda b_: (0, 0))],
        out_specs=pl.BlockSpec((1, T, W), lambda b_: (b_, 0, 0)),
        out_shape=jax.ShapeDtypeStruct((B, T, W), F32),
        compiler_params=_params("parallel"),
        name="fnet_dft_short",
    )(a, b, w)


def _merge_kernel(x_ref, sh_ref, sc_ref, gt_ref, g_ref, hf_ref, hb_ref, o_ref, mg_ref, sgu_ref, fn_ref,
                  wmg_ref, bmg_ref, wbr_ref, wout_ref, out_ref):
    x = x_ref[0]
    h = _norm_mod(x, g_ref[...], sh_ref[0], sc_ref[0]).astype(BF16)
    hs = hf_ref[0, 0] + hb_ref[0, 0]
    parts = []
    for hd in range(N_HEADS):
        z = hs[:, hd * HEAD_D:(hd + 1) * HEAD_D]
        mu = jnp.mean(z, axis=-1, keepdims=True)
        var = jnp.mean(jnp.square(z - mu), axis=-1, keepdims=True)
        parts.append((z - mu) * lax.rsqrt(var + EPS))
    hn = jnp.concatenate(parts, axis=1) * mg_ref[...]
    ym = jax.nn.sigmoid(o_ref[0]) * hn
    y = None
    for r, br in enumerate((ym, sgu_ref[0], fn_ref[0])):
        cols = slice(r * D_MODEL, (r + 1) * D_MODEL)
        gate = jax.nn.sigmoid(_dot(h, wmg_ref[:, cols]) + bmg_ref[:, cols])
        term = gate * _dot(br.astype(BF16), wbr_ref[r])
        y = term if y is None else y + term
    out_ref[0] = x + gt_ref[0] * _dot(y.astype(BF16), wout_ref[...])


def _merge(x, shift, scale, gate, g, hdir, qkvo, mnorm_g, sgu, fnet, wmg, bmg, wbr, wout):
    B, T, D = x.shape
    tm = min(256, T)
    tok = lambda b_, i: (b_, i, 0)
    per_b = lambda b_, i: (b_, 0, 0)
    c2 = lambda b_, i: (0, 0)
    return pl.pallas_call(
        _merge_kernel,
        grid=(B, T // tm),
        in_specs=[pl.BlockSpec((1, tm, D), tok),
                  pl.BlockSpec((1, 1, D), per_b), pl.BlockSpec((1, 1, D), per_b), pl.BlockSpec((1, 1, D), per_b),
                  pl.BlockSpec((1, D), c2),
                  pl.BlockSpec((1, 1, tm, MIX_W), lambda b_, i: (0, b_, i, 0)),
                  pl.BlockSpec((1, 1, tm, MIX_W), lambda b_, i: (1, b_, i, 0)),
                  pl.BlockSpec((1, tm, MIX_W), lambda b_, i: (b_, i, 3)),
                  pl.BlockSpec((1, MIX_W), c2),
                  pl.BlockSpec((1, tm, MIX_W), tok), pl.BlockSpec((1, tm, MIX_W), tok),
                  pl.BlockSpec((D, 3 * D), c2), pl.BlockSpec((1, 3 * D), c2),
                  pl.BlockSpec((3, MIX_W, D), lambda b_, i: (0, 0, 0)),
                  pl.BlockSpec((D, D), c2)],
        out_specs=pl.BlockSpec((1, tm, D), tok),
        out_shape=jax.ShapeDtypeStruct(x.shape, F32),
        compiler_params=_params("parallel", "parallel"),
        name="branch_merge",
    )(x, shift, scale, gate, g, hdir, hdir, qkvo, mnorm_g, sgu, fnet, wmg, bmg, wbr, wout)


def _top_rows(s, k):
    n = s.shape[1]
    row = lax.broadcasted_iota(jnp.int32, s.shape, 0).astype(F32)
    slot = lax.broadcasted_iota(jnp.int32, (k, n), 0)
    vals = jnp.zeros((k, n), F32)
    ids = jnp.zeros((k, n), F32)
    for i in range(k):
        m = jnp.max(s, axis=0, keepdims=True)
        idx = jnp.min(jnp.where(s == m, row, float(s.shape[0])), axis=0, keepdims=True)
        vals = jnp.where(slot == i, m, vals)
        ids = jnp.where(slot == i, idx, ids)
        s = jnp.where(row == idx, -jnp.inf, s)
    return vals, ids


def _topk_head(q_scr, keys_ref, idx_t_scr, w_t_scr, hd, t0, tm):
    n_half = TOPK // 2
    n_cand = TOPK + (TOPK - 1) * n_half
    n_exp = N_KEYS * N_KEYS
    tops = []
    for p in range(2):
        off = pl.multiple_of(hd * (2 * N_KEYS) + p * N_KEYS, N_KEYS)
        qhp = q_scr[t0:t0 + tm, pl.ds(off, N_KEYS)].astype(BF16)
        tops.append(_top_rows(_dot_nt(keys_ref[p], qhp), TOPK))
    (va, ia), (vb, ib) = tops
    cs = [va[0:1] + vb]
    ce = [ia[0:1] * N_KEYS + ib]
    for i in range(1, TOPK):
        cs.append(va[i:i + 1] + vb[0:n_half])
        ce.append(ia[i:i + 1] * N_KEYS + ib[0:n_half])
    cs = jnp.concatenate(cs, axis=0)
    row = lax.broadcasted_iota(jnp.int32, (n_cand, tm), 0).astype(F32)
    code = row * float(n_exp) + jnp.concatenate(ce, axis=0)
    slot = lax.broadcasted_iota(jnp.int32, (TOPK, tm), 0)
    best = jnp.zeros((TOPK, tm), F32)
    sel = jnp.zeros((TOPK, tm), F32)
    for i in range(TOPK):
        m = jnp.max(cs, axis=0, keepdims=True)
        first = jnp.min(jnp.where(cs == m, code, float(n_cand * n_exp)), axis=0, keepdims=True)
        best = jnp.where(slot == i, m, best)
        sel = jnp.where(slot == i, first, sel)
        cs = jnp.where(code == first, -jnp.inf, cs)
    sel = sel - float(n_exp) * jnp.floor(sel * (1.0 / n_exp))
    ex = jnp.exp(best - best[0:1])
    wts = ex / jnp.sum(ex, axis=0, keepdims=True)
    r0 = pl.multiple_of(hd * TOPK, TOPK)
    idx_t_scr[pl.ds(r0, TOPK), t0:t0 + tm] = sel
    w_t_scr[pl.ds(r0, TOPK), t0:t0 + tm] = wts


def _pack_table(tab):
    bits = lax.bitcast_convert_type(tab.astype(TABLE_DT), jnp.uint16).astype(jnp.uint32)
    word = bits[:, :HALF_D] | (bits[:, HALF_D:] << 16)
    return word.reshape(tab.shape[0] * SLAB, 128)


def _gather_rows(sidx_ref, slot, j, tab_ref, g_ref):
    for m in range(N_SEL):
        e = pl.multiple_of(sidx_ref[slot, j, m], SLAB)
        g_ref[pl.ds(m * SLAB, SLAB), :] = tab_ref[pl.ds(e, SLAB), :]
    chunks = [g_ref[pl.ds(c, N_SEL, stride=SLAB), :] for c in range(SLAB)]
    return pltpu.bitcast(jnp.concatenate(chunks, axis=1), TABLE_DT)


def _two_rows(r0, r1, width):
    sub = lax.broadcasted_iota(jnp.int32, (8, width), 0)
    return jnp.where(sub == 0, r0, jnp.where(sub == 1, r1, 0.0))


def _idx_copy(idx_hbm, sidx_ref, sem, row0, slot):
    return pltpu.make_async_copy(idx_hbm.at[pl.ds(row0, STAGE), :], sidx_ref.at[slot], sem.at[slot])


def _for_token_groups(idx_hbm, sidx_ref, sem, tm, process):
    i = pl.program_id(0)
    total = pl.num_programs(0) * tm
    tok0 = i * tm

    @pl.when(i == 0)
    def _():
        _idx_copy(idx_hbm, sidx_ref, sem, 0, 0).start()

    def stage(slot, base):
        for q in range(STAGE // GROUP):
            process(slot, q * GROUP, base + q * GROUP)

    def pair(k, carry):
        base = pl.multiple_of(k * (2 * STAGE), 2 * STAGE)
        _idx_copy(idx_hbm, sidx_ref, sem, 0, 0).wait()
        _idx_copy(idx_hbm, sidx_ref, sem, tok0 + base + STAGE, 1).start()
        stage(0, base)
        _idx_copy(idx_hbm, sidx_ref, sem, 0, 1).wait()
        nxt = tok0 + base + 2 * STAGE

        @pl.when(nxt < total)
        def _():
            _idx_copy(idx_hbm, sidx_ref, sem, nxt, 0).start()

        stage(1, base + STAGE)
        return carry

    lax.fori_loop(0, tm // (2 * STAGE), pair, 0)


def _peer_select_in_kernel(x_ref, sh_ref, sc_ref, g_ref, wq_ref, keys_ref, ex_ref, tab_ref, a_ref, idx_ref,
                           q_scr, idx_t_scr, w_t_scr, h2_scr, idx_scr, w_scr, g0_ref, g1_ref, act_ref,
                           sidx_ref, sem):
    i = pl.program_id(0)
    tm = x_ref.shape[0]
    n_it = tm // (2 * STAGE)
    hps = PEER_HEADS // (2 * n_it)
    assert hps * 2 * n_it == PEER_HEADS
    wr = i % 2
    rd = 1 - wr

    @pl.when(i == 0)
    def _():
        idx_scr[1] = jnp.zeros(idx_scr.shape[1:], jnp.int32)
        w_scr[1] = jnp.zeros(w_scr.shape[1:], F32)
        h2_scr[1] = jnp.zeros(h2_scr.shape[1:], F32)

    def stage_copy(row0, slot):
        return pltpu.make_async_copy(idx_scr.at[rd, pl.ds(row0, STAGE), :], sidx_ref.at[slot], sem.at[slot])

    stage_copy(0, 0).start()
    h2 = _norm_mod(x_ref[...], g_ref[...], sh_ref[0], sc_ref[0])
    h2_scr[wr] = h2
    q_scr[...] = _dot(h2.astype(BF16), wq_ref[...])

    lane = lax.broadcasted_iota(jnp.int32, (8, 2 * N_SEL), 1)
    even = (lane & 1) == 0
    sub = lax.broadcasted_iota(jnp.int32, (8, 2 * N_SEL), 0)

    def process(slot, j0, base):
        xg = h2_scr[rd, pl.ds(base, GROUP), :]
        rows = jnp.zeros((GROUP, 2 * N_SEL), F32)
        for j in range(GROUP):
            g = _gather_rows(sidx_ref, slot, j0 + j, tab_ref, g0_ref if j % 2 == 0 else g1_ref)
            lhs = _two_rows(xg[j:j + 1, :HALF_D], xg[j:j + 1, HALF_D:], HALF_D).astype(TABLE_DT)
            out = _dot_nt(lhs, g)
            z = jnp.where(even, out, pltpu.roll(out, 7, axis=0))
            s = z + pltpu.roll(z, 2 * N_SEL - 1, axis=1)
            rows = jnp.where(sub == j, s[0:1], rows)
        act_ref[pl.ds(base, GROUP), :] = rows

    def stage(slot, base):
        for q in range(STAGE // GROUP):
            process(slot, q * GROUP, base + q * GROUP)

    def retrieve(hd0):
        for k in range(hps):
            for t0 in range(0, tm, 128):
                _topk_head(q_scr, keys_ref, idx_t_scr, w_t_scr, hd0 + k, t0, 128)

    def body(it, carry):
        base = pl.multiple_of(it * (2 * STAGE), 2 * STAGE)
        stage_copy(0, 0).wait()
        stage_copy(base + STAGE, 1).start()
        retrieve(2 * hps * it)
        stage(0, base)
        stage_copy(0, 1).wait()

        @pl.when(it + 1 < n_it)
        def _():
            stage_copy(base + 2 * STAGE, 0).start()

        retrieve(2 * hps * it + hps)
        stage(1, base + STAGE)
        return carry

    lax.fori_loop(0, n_it, body, 0)
    ids = (idx_t_scr[...].T).astype(jnp.int32) * SLAB
    idx_scr[wr] = ids
    idx_ref[...] = ids
    w_scr[wr] = w_t_scr[...].T
    w2 = _dot_exact(w_scr[rd], ex_ref[...])
    a_ref[...] = jax.nn.gelu(act_ref[...]) * w2


def _peer_v_kernel(idx_hbm, a_ref, x_ref, gt_ref, tab_ref, o_ref, g0_ref, g1_ref, sidx_ref, sem):
    tm = a_ref.shape[0]
    sub = lax.broadcasted_iota(jnp.int32, (8, D_MODEL), 0)

    def process(slot, j0, base):
        ag = a_ref[pl.ds(base, GROUP), :]
        rows = jnp.zeros((GROUP, D_MODEL), F32)
        for j in range(GROUP):
            g = _gather_rows(sidx_ref, slot, j0 + j, tab_ref, g0_ref if j % 2 == 0 else g1_ref)
            a0 = jnp.broadcast_to(ag[j:j + 1, :], (8, 2 * N_SEL))
            lhs = _two_rows(a0, pltpu.roll(a0, 1, axis=1), 2 * N_SEL).astype(TABLE_DT)
            out = _dot(lhs, g)
            rows = jnp.where(sub == j, jnp.concatenate([out[0:1], out[1:2]], axis=1), rows)
        o_ref[pl.ds(base, GROUP), :] = x_ref[pl.ds(base, GROUP), :] + gt_ref[0] * rows

    _for_token_groups(idx_hbm, sidx_ref, sem, tm, process)


def _peer_ffn(x, shift, scale, gate, g, wq, keys, utab, vtab):
    B, T, D = x.shape
    N = B * T
    expand = np.zeros((N_SEL, 2 * N_SEL), np.float32)
    expand[np.arange(N_SEL), 2 * np.arange(N_SEL)] = 1.0
    flat = lambda i: (i, 0)
    gscr = pltpu.VMEM((N_SEL * SLAB, 128), jnp.uint32)
    stage = [pltpu.SMEM((2, STAGE, N_SEL), jnp.int32), pltpu.SemaphoreType.DMA((2,))]
    x2 = x.reshape(N, D)

    ts = 256
    nt = N // ts
    tpb_s = T // ts
    WQ = wq.shape[1]
    tile = lambda i: jnp.minimum(i, nt - 1)
    a, idx2 = pl.pallas_call(
        _peer_select_in_kernel,
        grid=(nt + 1,),
        in_specs=[pl.BlockSpec((ts, D), lambda i: (tile(i), 0)),
                  pl.BlockSpec((1, 1, D), lambda i: (tile(i) // tpb_s, 0, 0)),
                  pl.BlockSpec((1, 1, D), lambda i: (tile(i) // tpb_s, 0, 0)),
                  pl.BlockSpec((1, D), lambda i: (0, 0)),
                  pl.BlockSpec((D, WQ), lambda i: (0, 0)),
                  pl.BlockSpec((2, N_KEYS, N_KEYS), lambda i: (0, 0, 0)),
                  pl.BlockSpec((N_SEL, 2 * N_SEL), lambda i: (0, 0)),
                  pl.BlockSpec(memory_space=pltpu.VMEM)],
        out_specs=[pl.BlockSpec((ts, 2 * N_SEL), lambda i: (jnp.maximum(i - 1, 0), 0)),
                   pl.BlockSpec((ts, N_SEL), lambda i: (tile(i), 0))],
        out_shape=[jax.ShapeDtypeStruct((N, 2 * N_SEL), F32), jax.ShapeDtypeStruct((N, N_SEL), jnp.int32)],
        scratch_shapes=[pltpu.VMEM((ts, WQ), F32), pltpu.VMEM((N_SEL, ts), F32), pltpu.VMEM((N_SEL, ts), F32),
                        pltpu.VMEM((2, ts, D), F32), pltpu.VMEM((2, ts, N_SEL), jnp.int32),
                        pltpu.VMEM((2, ts, N_SEL), F32), gscr, gscr, pltpu.VMEM((ts, 2 * N_SEL), F32)] + stage,
        compiler_params=_params("arbitrary"),
        name="peer_select_in",
    )(x2, shift, scale, g, wq, keys, jnp.asarray(expand), utab)

    tm = 128
    tpb = T // tm
    out = pl.pallas_call(
        _peer_v_kernel,
        grid=(N // tm,),
        in_specs=[pl.BlockSpec(memory_space=pl.ANY),
                  pl.BlockSpec((tm, 2 * N_SEL), flat),
                  pl.BlockSpec((tm, D), flat),
                  pl.BlockSpec((1, 1, D), lambda i: (i // tpb, 0, 0)),
                  pl.BlockSpec(memory_space=pltpu.VMEM)],
        out_specs=pl.BlockSpec((tm, D), flat),
        out_shape=jax.ShapeDtypeStruct((N, D), F32),
        scratch_shapes=[gscr, gscr] + stage,
        compiler_params=_params("arbitrary"),
        name="peer_expert_out",
    )(idx2, a, x2, gate, vtab)
    return out.reshape(B, T, D)


def _halo_rows(qk, C):
    B, T, _ = qk.shape
    z = jnp.zeros((B, 1, C), qk.dtype)
    prev = jnp.concatenate([z, qk[:, CHUNK - 1:T - 1:CHUNK, :C]], axis=1)
    nxt = jnp.concatenate([qk[:, CHUNK::CHUNK, :C], z], axis=1)
    return jnp.stack([prev, nxt], axis=2)


def _stream_layer(x, mod, lw, init):
    B, T, D = x.shape
    sh1, sc1, gt1, sh2, sc2, gt2 = mod
    qkvo, gd, gd_t = _mlstm_proj(x, sh1, sc1, lw["g1"], lw["w_qkvo"], lw["b_qkvo"], lw["w_gate"], lw["b_gate"])
    sgu, fa, fb = _sgu_fnet_proj(x, sh1, sc1, lw["g1"], lw["w_sf"], lw["b_sf"], lw["sgu_w"], lw["sgu_bt"],
                                 lw["dft_c"], lw["dft_s"])
    halo = _halo_rows(qkvo, 2 * MIX_W)
    hdir, c_fin, n_fin, m_fin = _mlstm_scan(qkvo, halo, gd, gd_t, lw["conv"], *init)
    fnet = _fnet_seq_dft_long(fa, fb) if T > 2 * CHUNK else _fnet_seq_dft_short(fa, fb)
    x = _merge(x, sh1, sc1, gt1, lw["g1"], hdir, qkvo, lw["mnorm_g"], sgu, fnet,
               lw["w_mg"], lw["b_mg"], lw["w_br"], lw["w_out"])
    x = _peer_ffn(x, sh2, sc2, gt2, lw["g2"], lw["wq"], lw["keys"], lw["utab"], lw["vtab"])
    return x, (c_fin, n_fin, m_fin)


def _pos_embed(rows):
    quarter = D_MODEL // 4
    omega = 1.0 / (10000.0 ** (jnp.arange(quarter, dtype=F32) / quarter))
    r = jnp.repeat(jnp.arange(rows, dtype=F32), GRID_W)[:, None] * omega
    cc = jnp.tile(jnp.arange(GRID_W, dtype=F32), rows)[:, None] * omega
    return jnp.concatenate([jnp.sin(r), jnp.cos(r), jnp.sin(cc), jnp.cos(cc)], axis=-1)


def kernel(x, c, ctx, c_ctx, w_mod, b_mod, norm1_g, norm2_g, w_in, b_in, conv_qk, mlstm_norm_g, sgu_w, sgu_b, w_br, w_out, peer_wq, peer_keys, peer_u, peer_v, final_g):
    B, T, D = x.shape
    depth = w_mod.shape[0]
    x = _add_pos(x, _pos_embed(T // GRID_W))

    cc = jnp.zeros((8, D), F32).at[:B].set(c).at[B].set(c_ctx)
    mods = _modulation(cc, w_mod, b_mod)

    cdft, sdft = _dft_mats(GROUP_W)
    dft_c = jnp.asarray(cdft, BF16)
    dft_s = jnp.asarray(sdft, BF16)
    o_q, o_g = 4 * MIX_W, 4 * MIX_W + 4 * N_HEADS
    o_sf, o_mg = o_g, o_g + 3 * MIX_W

    def layer(carry, lp):
        x, ctx = carry
        (mod, g1, g2, win, bin_, conv, mng, sw, sb, wbr, wout, wq, keys, pu, pv) = lp
        lw = {
            "g1": g1.reshape(1, D), "g2": g2.reshape(1, D),
            "w_qkvo": win[:, :o_q].astype(BF16), "b_qkvo": bin_[:o_q].reshape(1, -1),
            "w_gate": jnp.pad(win[:, o_q:o_g].reshape(D, 2, 2 * N_HEADS),
                              ((0, 0), (0, 0), (0, 128 - 2 * N_HEADS))).reshape(D, 256),
            "b_gate": jnp.pad(bin_[o_q:o_g].reshape(2, 2 * N_HEADS), ((0, 0), (0, 128 - 2 * N_HEADS))).reshape(1, 256),
            "w_sf": win[:, o_sf:o_mg].astype(BF16), "b_sf": bin_[o_sf:o_mg].reshape(1, -1),
            "w_mg": win[:, o_mg:].astype(BF16), "b_mg": bin_[o_mg:].reshape(1, -1),
            "conv": conv, "mnorm_g": mng.reshape(1, -1),
            "sgu_w": sw.astype(BF16), "sgu_bt": sb.T,
            "dft_c": dft_c, "dft_s": dft_s,
            "w_br": wbr.astype(BF16), "w_out": wout.astype(BF16),
            "wq": wq.astype(BF16), "keys": keys.astype(BF16),
            "utab": _pack_table(pu), "vtab": _pack_table(pv),
        }
        mod_x = [mod[:B, i * D:(i + 1) * D].reshape(B, 1, D) for i in range(N_MOD)]
        mod_c = [jnp.broadcast_to(mod[B, i * D:(i + 1) * D].reshape(1, 1, D), (B, 1, D)) for i in range(N_MOD)]
        zero = (jnp.zeros((2, B, N_HEADS, HEAD_D, HEAD_D), F32),
                jnp.zeros((2, B, N_HEADS, 1, HEAD_D), F32),
                jnp.zeros((2, B, N_HEADS, 1, HEAD_D), F32))
        ctx_new, ctx_state = _stream_layer(ctx, mod_c, lw, zero)
        x_new, _ = _stream_layer(x, mod_x, lw, ctx_state)
        return (x_new, ctx_new), None

    xs = (mods, norm1_g, norm2_g, w_in, b_in, conv_qk, mlstm_norm_g, sgu_w, sgu_b, w_br, w_out,
          peer_wq, peer_keys, peer_u, peer_v)
    (x, ctx), _ = lax.scan(layer, (x, ctx), xs)
    return _final_norm(x, final_g)
```

```python
import numpy as np
import jax
import jax.numpy as jnp
from jax import lax
from jax.experimental import pallas as pl
from jax.experimental.pallas import tpu as pltpu

F32 = jnp.float32
BF16 = jnp.bfloat16
TABLE_DT = jnp.bfloat16
HIGHEST = lax.Precision.HIGHEST

D_MODEL = 1024
GRID_W = 64
EPS = 1e-6
N_MOD = 6
N_HEADS = 4
HEAD_D = 128
CHUNK = 128
MIX_W = 512
N_GROUPS = 4
GROUP_W = 128
PEER_HEADS = 8
N_KEYS = 128
TOPK = 16
N_SEL = PEER_HEADS * TOPK
HALF_D = D_MODEL // 2
SLAB = 4
GROUP = 8
STAGE = 64
VMEM_LIMIT = 56 * 1024 * 1024


def _params(*sem):
    return pltpu.CompilerParams(dimension_semantics=sem, vmem_limit_bytes=VMEM_LIMIT)


def _dot(a, b):
    return jnp.dot(a, b, preferred_element_type=F32)


def _dot_nt(a, b):
    return lax.dot_general(a, b, (((1,), (1,)), ((), ())), preferred_element_type=F32)


def _dot_tn(a, b):
    return lax.dot_general(a, b, (((0,), (0,)), ((), ())), preferred_element_type=F32)


def _dot_exact(a, b):
    return jnp.dot(a, b, preferred_element_type=F32, precision=HIGHEST)


def _norm_mod(x, g, shift, scale):
    y = x * lax.rsqrt(jnp.mean(x * x, axis=-1, keepdims=True) + EPS)
    return (y * g) * (1.0 + scale) + shift


def _add_pos_kernel(x_ref, p_ref, o_ref):
    o_ref[0] = x_ref[0] + p_ref[...]


def _add_pos(x, pos):
    B, T, D = x.shape
    tm = 512
    return pl.pallas_call(
        _add_pos_kernel,
        grid=(B, T // tm),
        in_specs=[pl.BlockSpec((1, tm, D), lambda b, i: (b, i, 0)),
                  pl.BlockSpec((tm, D), lambda b, i: (i, 0))],
        out_specs=pl.BlockSpec((1, tm, D), lambda b, i: (b, i, 0)),
        out_shape=jax.ShapeDtypeStruct(x.shape, x.dtype),
        compiler_params=_params("parallel", "parallel"),
        name="add_pos",
    )(x, pos)


def _final_norm_kernel(x_ref, g_ref, o_ref):
    x = x_ref[0]
    o_ref[0] = x * lax.rsqrt(jnp.mean(x * x, axis=-1, keepdims=True) + EPS) * g_ref[...]


def _final_norm(x, g):
    B, T, D = x.shape
    tm = 512
    return pl.pallas_call(
        _final_norm_kernel,
        grid=(B, T // tm),
        in_specs=[pl.BlockSpec((1, tm, D), lambda b, i: (b, i, 0)),
                  pl.BlockSpec((1, D), lambda b, i: (0, 0))],
        out_specs=pl.BlockSpec((1, tm, D), lambda b, i: (b, i, 0)),
        out_shape=jax.ShapeDtypeStruct(x.shape, x.dtype),
        compiler_params=_params("parallel", "parallel"),
        name="final_norm",
    )(x, g.reshape(1, D))


def _mod_kernel(c_ref, w_ref, b_ref, o_ref):
    c = c_ref[...]
    s = c * jax.nn.sigmoid(c)
    o_ref[0] = _dot_exact(s, w_ref[0]) + b_ref[0]


def _modulation(cc, w_mod, b_mod):
    depth, D, W = w_mod.shape
    return pl.pallas_call(
        _mod_kernel,
        grid=(depth, W // D),
        in_specs=[pl.BlockSpec((8, D), lambda l, j: (0, 0)),
                  pl.BlockSpec((1, D, D), lambda l, j: (l, 0, j)),
                  pl.BlockSpec((1, 1, D), lambda l, j: (l, 0, j))],
        out_specs=pl.BlockSpec((1, 8, D), lambda l, j: (l, 0, j)),
        out_shape=jax.ShapeDtypeStruct((depth, 8, W), F32),
        compiler_params=_params("parallel", "parallel"),
        name="adaln_mod",
    )(cc, w_mod, b_mod.reshape(depth, 1, W))


def _mproj_kernel(x_ref, sh_ref, sc_ref, g_ref, w_ref, b_ref, wg_ref, bg_ref, o_ref, og_ref, ogt_ref):
    h = _norm_mod(x_ref[0], g_ref[...], sh_ref[0], sc_ref[0])
    o_ref[0] = _dot(h.astype(BF16), w_ref[...]) + b_ref[...]
    gates = _dot_exact(h, wg_ref[...]) + bg_ref[...]
    for d in range(2):
        gd = gates[:, d * 128:(d + 1) * 128]
        og_ref[d, 0] = gd
        ogt_ref[d, 0] = jnp.transpose(gd)[:2 * N_HEADS, :]


def _mlstm_proj(x, shift, scale, g, w, b, wg, bg):
    B, T, D = x.shape
    tm = min(512, T)
    W = w.shape[1]
    NG = wg.shape[1]
    tok = lambda b_, i: (b_, i, 0)
    per_b = lambda b_, i: (b_, 0, 0)
    const = lambda b_, i: (0, 0)
    return pl.pallas_call(
        _mproj_kernel,
        grid=(B, T // tm),
        in_specs=[pl.BlockSpec((1, tm, D), tok),
                  pl.BlockSpec((1, 1, D), per_b), pl.BlockSpec((1, 1, D), per_b),
                  pl.BlockSpec((1, D), const),
                  pl.BlockSpec((D, W), const), pl.BlockSpec((1, W), const),
                  pl.BlockSpec((D, NG), const), pl.BlockSpec((1, NG), const)],
        out_specs=[pl.BlockSpec((1, tm, W), tok),
                   pl.BlockSpec((2, 1, tm, 128), lambda b_, i: (0, b_, i, 0)),
                   pl.BlockSpec((2, 1, 2 * N_HEADS, tm), lambda b_, i: (0, b_, 0, i))],
        out_shape=[jax.ShapeDtypeStruct((B, T, W), F32),
                   jax.ShapeDtypeStruct((2, B, T, 128), F32),
                   jax.ShapeDtypeStruct((2, B, 2 * N_HEADS, T), F32)],
        compiler_params=_params("parallel", "parallel"),
        name="mlstm_proj",
    )(x, shift, scale, g, w, b, wg, bg)


def _sgu_fnet_kernel(x_ref, sh_ref, sc_ref, g_ref, w_ref, b_ref, sw_ref, sbt_ref, cc_ref, cs_ref,
                     sgu_ref, a_ref, bm_ref):
    h = _norm_mod(x_ref[0], g_ref[...], sh_ref[0], sc_ref[0])
    p = _dot(h.astype(BF16), w_ref[...]) + b_ref[...]
    tm = p.shape[0]
    u = jax.nn.gelu(p[:, :MIX_W])
    vf = jax.nn.gelu(p[:, MIX_W:2 * MIX_W])
    mu = jnp.mean(vf, axis=-1, keepdims=True)
    var = jnp.mean(jnp.square(vf - mu), axis=-1, keepdims=True)
    v = ((vf - mu) * lax.rsqrt(var + EPS)).astype(BF16)
    for j in range(tm // CHUNK):
        rows = slice(j * CHUNK, (j + 1) * CHUNK)
        for gi in range(N_GROUPS):
            cols = slice(gi * GROUP_W, (gi + 1) * GROUP_W)
            s = _dot(sw_ref[gi], v[rows, cols]) + sbt_ref[:, gi:gi + 1]
            sgu_ref[0, rows, cols] = u[rows, cols] * s
    fz = p[:, 2 * MIX_W:].astype(BF16)
    for gi in range(N_GROUPS):
        cols = slice(gi * GROUP_W, (gi + 1) * GROUP_W)
        a_ref[0, :, cols] = _dot(fz[:, cols], cc_ref[...])
        bm_ref[0, :, cols] = _dot(fz[:, cols], cs_ref[...])


def _sgu_fnet_proj(x, shift, scale, g, w, b, sw, sbt, cc, cs):
    B, T, D = x.shape
    tm = min(512, T)
    W = w.shape[1]
    tok = lambda b_, i: (b_, i, 0)
    per_b = lambda b_, i: (b_, 0, 0)
    c2 = lambda b_, i: (0, 0)
    c3 = lambda b_, i: (0, 0, 0)
    out = jax.ShapeDtypeStruct((B, T, MIX_W), F32)
    return pl.pallas_call(
        _sgu_fnet_kernel,
        grid=(B, T // tm),
        in_specs=[pl.BlockSpec((1, tm, D), tok),
                  pl.BlockSpec((1, 1, D), per_b), pl.BlockSpec((1, 1, D), per_b),
                  pl.BlockSpec((1, D), c2),
                  pl.BlockSpec((D, W), c2), pl.BlockSpec((1, W), c2),
                  pl.BlockSpec((N_GROUPS, CHUNK, CHUNK), c3), pl.BlockSpec((CHUNK, N_GROUPS), c2),
                  pl.BlockSpec((GROUP_W, GROUP_W), c2), pl.BlockSpec((GROUP_W, GROUP_W), c2)],
        out_specs=[pl.BlockSpec((1, tm, MIX_W), tok)] * 3,
        out_shape=[out, out, out],
        compiler_params=_params("parallel", "parallel"),
        name="sgu_fnet_proj",
    )(x, shift, scale, g, w, b, sw, sbt, cc, cs)


def _log_sigmoid(x):
    return jnp.minimum(x, 0.0) - jnp.log1p(jnp.exp(-jnp.abs(x)))


def _mlstm_kernel(q_ref, k_ref, v_ref, halo_ref, gt_ref, gtt_ref, cw_ref, c0_ref, n0_ref, m0_ref,
                  h_ref, c_ref, n_ref, m_ref):
    d = pl.program_id(0)
    c = pl.program_id(2)

    @pl.when(c == 0)
    def _():
        c_ref[0, 0] = c0_ref[0, 0]
        n_ref[0, 0] = n0_ref[0, 0]
        m_ref[0, 0] = m0_ref[0, 0]

    row = lax.broadcasted_iota(jnp.int32, (CHUNK, CHUNK), 0)
    col = lax.broadcasted_iota(jnp.int32, (CHUNK, CHUNK), 1)
    fwd = d == 0
    sgn = jnp.where(fwd, 1, -1)
    mask = (col - row) * sgn <= 0
    tri = mask.astype(F32)
    tri_t = ((row - col) * sgn <= 0).astype(F32)

    W = N_HEADS * HEAD_D
    ridx = lax.broadcasted_iota(jnp.int32, (CHUNK, 2 * W), 0)
    qk = jnp.concatenate([q_ref[0], k_ref[0]], axis=1)
    prev = jnp.where(ridx == 0, halo_ref[0, 0, 0:1, :], pltpu.roll(qk, 1, axis=0))
    nxt = jnp.where(ridx == CHUNK - 1, halo_ref[0, 0, 1:2, :], pltpu.roll(qk, CHUNK - 1, axis=0))
    qk = prev * cw_ref[0:1, :] + qk * cw_ref[1:2, :] + nxt * cw_ref[2:3, :]
    qk = qk * jax.nn.sigmoid(qk)

    gates = gt_ref[0, 0]
    gates_t = gtt_ref[0, 0]
    b_cols = _dot_exact(tri, _log_sigmoid(gates))
    b_rows = _dot_exact(_log_sigmoid(gates_t), tri_t)
    vv = v_ref[0]

    for hd in range(N_HEADS):
        lanes = slice(hd * HEAD_D, (hd + 1) * HEAD_D)
        qf = qk[:, lanes]
        kf = qk[:, W + hd * HEAD_D:W + (hd + 1) * HEAD_D] * (HEAD_D ** -0.5)
        q = qf.astype(BF16)
        v = vv[:, lanes].astype(BF16)
        i_col = gates[:, hd:hd + 1]
        i_row = gates_t[hd:hd + 1, :]
        b_col = b_cols[:, N_HEADS + hd:N_HEADS + hd + 1]
        b_row = b_rows[N_HEADS + hd:N_HEADS + hd + 1, :]
        b_last = jnp.where(fwd, b_row[:, CHUNK - 1:CHUNK], b_row[:, 0:1])

        c_st = c_ref[0, 0, hd]
        n_st = n_ref[0, 0, hd]
        m_st = m_ref[0, 0, hd][:, 0:1]

        dmat = jnp.where(mask, b_col - b_row + i_row, -jnp.inf)
        inter = b_col + m_st
        m_t = jnp.maximum(inter, jnp.max(dmat, axis=-1, keepdims=True))
        s = _dot_nt(q, kf.astype(BF16)) * jnp.exp(dmat - m_t)
        a = jnp.exp(inter - m_t)
        num = _dot(s.astype(BF16), v) + a * _dot(q, c_st.astype(BF16))
        den = jnp.sum(s, axis=-1, keepdims=True) + a * jnp.sum(qf * n_st, axis=-1, keepdims=True)
        h_ref[0, 0, :, lanes] = num / jnp.maximum(jnp.abs(den), jnp.exp(-m_t))

        g_row = b_last - b_row + i_row
        g_col = b_last - b_col + i_col
        m_new = jnp.maximum(b_last + m_st, jnp.max(g_row, axis=-1, keepdims=True))
        kw = kf * jnp.exp(g_col - m_new)
        decay = jnp.exp(b_last + m_st - m_new)
        c_ref[0, 0, hd] = decay * c_st + _dot(jnp.transpose(kw).astype(BF16), v)
        n_ref[0, 0, hd] = decay * n_st + jnp.sum(kw, axis=0, keepdims=True)
        m_ref[0, 0, hd] = jnp.broadcast_to(m_new, (1, HEAD_D))


def _mlstm_scan(qkvo, halo, gates, gates_t, conv_w, c0, n0, m0):
    B, T, _ = qkvo.shape
    nc = T // CHUNK
    W = N_HEADS * HEAD_D

    def chunk(d, c):
        return c + d * (nc - 1 - 2 * c)

    st = lambda d, b, c: (d, b, 0, 0, 0)
    return pl.pallas_call(
        _mlstm_kernel,
        grid=(2, B, nc),
        in_specs=[pl.BlockSpec((1, CHUNK, W), lambda d, b, c: (b, chunk(d, c), 0)),
                  pl.BlockSpec((1, CHUNK, W), lambda d, b, c: (b, chunk(d, c), 1)),
                  pl.BlockSpec((1, CHUNK, W), lambda d, b, c: (b, chunk(d, c), 2)),
                  pl.BlockSpec((1, 1, 2, 2 * W), lambda d, b, c: (b, chunk(d, c), 0, 0)),
                  pl.BlockSpec((1, 1, CHUNK, 128), lambda d, b, c: (d, b, chunk(d, c), 0)),
                  pl.BlockSpec((1, 1, 2 * N_HEADS, CHUNK), lambda d, b, c: (d, b, 0, chunk(d, c))),
                  pl.BlockSpec((3, 2 * W), lambda d, b, c: (0, 0)),
                  pl.BlockSpec((1, 1, N_HEADS, HEAD_D, HEAD_D), st),
                  pl.BlockSpec((1, 1, N_HEADS, 1, HEAD_D), st),
                  pl.BlockSpec((1, 1, N_HEADS, 1, HEAD_D), st)],
        out_specs=[pl.BlockSpec((1, 1, CHUNK, W), lambda d, b, c: (d, b, chunk(d, c), 0)),
                   pl.BlockSpec((1, 1, N_HEADS, HEAD_D, HEAD_D), st),
                   pl.BlockSpec((1, 1, N_HEADS, 1, HEAD_D), st),
                   pl.BlockSpec((1, 1, N_HEADS, 1, HEAD_D), st)],
        out_shape=[jax.ShapeDtypeStruct((2, B, T, W), F32),
                   jax.ShapeDtypeStruct((2, B, N_HEADS, HEAD_D, HEAD_D), F32),
                   jax.ShapeDtypeStruct((2, B, N_HEADS, 1, HEAD_D), F32),
                   jax.ShapeDtypeStruct((2, B, N_HEADS, 1, HEAD_D), F32)],
        compiler_params=_params("parallel", "parallel", "arbitrary"),
        name="mlstm_scan",
    )(qkvo, qkvo, qkvo, halo, gates, gates_t, conv_w, c0, n0, m0)


def _dft_mats(n):
    j = np.arange(n, dtype=np.float64)
    ang = 2.0 * np.pi * np.outer(j, j) / n
    return np.cos(ang) / np.sqrt(n), np.sin(ang) / np.sqrt(n)


def _fft_stage1_kernel(a_ref, b_ref, w_ref, o_ref):
    ab = jnp.concatenate([a_ref[0], b_ref[0]], axis=0).astype(BF16)
    o_ref[0] = _dot(w_ref[...], ab)


def _fft_stage2_kernel(x_ref, tc_ref, ts_ref, w_ref, o_ref):
    nk = x_ref.shape[2]
    for j in range(nk):
        xr = x_ref[0, 0, j]
        xi = x_ref[0, 1, j]
        cw = jnp.concatenate([tc_ref[j]] * N_GROUPS, axis=1)
        sw = jnp.concatenate([ts_ref[j]] * N_GROUPS, axis=1)
        yr = xr * cw + xi * sw
        yi = xi * cw - xr * sw
        y = jnp.concatenate([yr, yi], axis=0).astype(BF16)
        o_ref[0, :, j, :] = _dot(w_ref[...], y)


def _fnet_seq_dft_long(a, b):
    B, T, W = a.shape
    T2 = CHUNK
    T1 = T // T2
    c1, s1 = _dft_mats(T1)
    c2, s2 = _dft_mats(T2)
    w1 = jnp.asarray(np.block([[c1, -s1], [-s1, -c1]]), BF16)
    w2 = jnp.asarray(np.concatenate([c2, s2], axis=1), BF16)
    ang = (2.0 * np.pi / T) * np.outer(np.arange(T1), np.arange(T2))
    tw_c = jnp.broadcast_to(jnp.asarray(np.cos(ang), F32)[:, :, None], (T1, T2, GROUP_W))
    tw_s = jnp.broadcast_to(jnp.asarray(np.sin(ang), F32)[:, :, None], (T1, T2, GROUP_W))

    ncol = T2 * W
    cb = ncol // 4
    x1 = pl.pallas_call(
        _fft_stage1_kernel,
        grid=(B, 4),
        in_specs=[pl.BlockSpec((1, T1, cb), lambda b_, j: (b_, 0, j)),
                  pl.BlockSpec((1, T1, cb), lambda b_, j: (b_, 0, j)),
                  pl.BlockSpec((2 * T1, 2 * T1), lambda b_, j: (0, 0))],
        out_specs=pl.BlockSpec((1, 2 * T1, cb), lambda b_, j: (b_, 0, j)),
        out_shape=jax.ShapeDtypeStruct((B, 2 * T1, ncol), F32),
        compiler_params=_params("parallel", "parallel"),
        name="fnet_dft_stage1",
    )(a.reshape(B, T1, ncol), b.reshape(B, T1, ncol), w1)

    nk = min(8, T1)
    y = pl.pallas_call(
        _fft_stage2_kernel,
        grid=(B, T1 // nk),
        in_specs=[pl.BlockSpec((1, 2, nk, T2, W), lambda b_, j: (b_, 0, j, 0, 0)),
                  pl.BlockSpec((nk, T2, GROUP_W), lambda b_, j: (j, 0, 0)),
                  pl.BlockSpec((nk, T2, GROUP_W), lambda b_, j: (j, 0, 0)),
                  pl.BlockSpec((T2, 2 * T2), lambda b_, j: (0, 0))],
        out_specs=pl.BlockSpec((1, T2, nk, W), lambda b_, j: (b_, 0, j, 0)),
        out_shape=jax.ShapeDtypeStruct((B, T2, T1, W), F32),
        compiler_params=_params("parallel", "parallel"),
        name="fnet_dft_stage2",
    )(x1.reshape(B, 2, T1, T2, W), tw_c, tw_s, w2)
    return y.reshape(B, T, W)


def _dft_short_kernel(a_ref, b_ref, w_ref, o_ref):
    ab = jnp.concatenate([a_ref[0], b_ref[0]], axis=0).astype(BF16)
    o_ref[0] = _dot(w_ref[...], ab)


def _fnet_seq_dft_short(a, b):
    B, T, W = a.shape
    c, s = _dft_mats(T)
    w = jnp.asarray(np.concatenate([c, -s], axis=1), BF16)
    return pl.pallas_call(
        _dft_short_kernel,
        grid=(B,),
        in_specs=[pl.BlockSpec((1, T, W), lambda b_: (b_, 0, 0)),
                  pl.BlockSpec((1, T, W), lambda b_: (b_, 0, 0)),
                  pl.BlockSpec((T, 2 * T), lambda b_: (0, 0))],
        out_specs=pl.BlockSpec((1, T, W), lambda b_: (b_, 0, 0)),
        out_shape=jax.ShapeDtypeStruct((B, T, W), F32),
        compiler_params=_params("parallel"),
        name="fnet_dft_short",
    )(a, b, w)


def _merge_kernel(x_ref, sh_ref, sc_ref, gt_ref, g_ref, hf_ref, hb_ref, o_ref, mg_ref, sgu_ref, fn_ref,
                  wmg_ref, bmg_ref, wbr_ref, wout_ref, out_ref):
    x = x_ref[0]
    h = _norm_mod(x, g_ref[...], sh_ref[0], sc_ref[0]).astype(BF16)
    hs = hf_ref[0, 0] + hb_ref[0, 0]
    parts = []
    for hd in range(N_HEADS):
        z = hs[:, hd * HEAD_D:(hd + 1) * HEAD_D]
        mu = jnp.mean(z, axis=-1, keepdims=True)
        var = jnp.mean(jnp.square(z - mu), axis=-1, keepdims=True)
        parts.append((z - mu) * lax.rsqrt(var + EPS))
    hn = jnp.concatenate(parts, axis=1) * mg_ref[...]
    ym = jax.nn.sigmoid(o_ref[0]) * hn
    y = None
    for r, br in enumerate((ym, sgu_ref[0], fn_ref[0])):
        cols = slice(r * D_MODEL, (r + 1) * D_MODEL)
        gate = jax.nn.sigmoid(_dot(h, wmg_ref[:, cols]) + bmg_ref[:, cols])
        term = gate * _dot(br.astype(BF16), wbr_ref[r])
        y = term if y is None else y + term
    out_ref[0] = x + gt_ref[0] * _dot(y.astype(BF16), wout_ref[...])


def _merge(x, shift, scale, gate, g, hdir, qkvo, mnorm_g, sgu, fnet, wmg, bmg, wbr, wout):
    B, T, D = x.shape
    tm = min(256, T)
    tok = lambda b_, i: (b_, i, 0)
    per_b = lambda b_, i: (b_, 0, 0)
    c2 = lambda b_, i: (0, 0)
    return pl.pallas_call(
        _merge_kernel,
        grid=(B, T // tm),
        in_specs=[pl.BlockSpec((1, tm, D), tok),
                  pl.BlockSpec((1, 1, D), per_b), pl.BlockSpec((1, 1, D), per_b), pl.BlockSpec((1, 1, D), per_b),
                  pl.BlockSpec((1, D), c2),
                  pl.BlockSpec((1, 1, tm, MIX_W), lambda b_, i: (0, b_, i, 0)),
                  pl.BlockSpec((1, 1, tm, MIX_W), lambda b_, i: (1, b_, i, 0)),
                  pl.BlockSpec((1, tm, MIX_W), lambda b_, i: (b_, i, 3)),
                  pl.BlockSpec((1, MIX_W), c2),
                  pl.BlockSpec((1, tm, MIX_W), tok), pl.BlockSpec((1, tm, MIX_W), tok),
                  pl.BlockSpec((D, 3 * D), c2), pl.BlockSpec((1, 3 * D), c2),
                  pl.BlockSpec((3, MIX_W, D), lambda b_, i: (0, 0, 0)),
                  pl.BlockSpec((D, D), c2)],
        out_specs=pl.BlockSpec((1, tm, D), tok),
        out_shape=jax.ShapeDtypeStruct(x.shape, F32),
        compiler_params=_params("parallel", "parallel"),
        name="branch_merge",
    )(x, shift, scale, gate, g, hdir, hdir, qkvo, mnorm_g, sgu, fnet, wmg, bmg, wbr, wout)


def _top_rows(s, k):
    n = s.shape[1]
    row = lax.broadcasted_iota(jnp.int32, s.shape, 0).astype(F32)
    slot = lax.broadcasted_iota(jnp.int32, (k, n), 0)
    vals = jnp.zeros((k, n), F32)
    ids = jnp.zeros((k, n), F32)
    for i in range(k):
        m = jnp.max(s, axis=0, keepdims=True)
        idx = jnp.min(jnp.where(s == m, row, float(s.shape[0])), axis=0, keepdims=True)
        vals = jnp.where(slot == i, m, vals)
        ids = jnp.where(slot == i, idx, ids)
        s = jnp.where(row == idx, -jnp.inf, s)
    return vals, ids


def _topk_head(q_scr, keys_ref, idx_t_scr, w_t_scr, hd, t0, tm, after=None):
    n_half = TOPK // 2
    n_cand = TOPK + (TOPK - 1) * n_half
    n_exp = N_KEYS * N_KEYS
    tops = []
    for p in range(2):
        off = pl.multiple_of(hd * (2 * N_KEYS) + p * N_KEYS, N_KEYS)
        qhp = q_scr[t0:t0 + tm, pl.ds(off, N_KEYS)].astype(BF16)
        scores = _dot_nt(keys_ref[p], qhp)
        if after is not None:
            scores = scores + after
        tops.append(_top_rows(scores, TOPK))
    (va, ia), (vb, ib) = tops
    cs = [va[0:1] + vb]
    ce = [ia[0:1] * N_KEYS + ib]
    for i in range(1, TOPK):
        cs.append(va[i:i + 1] + vb[0:n_half])
        ce.append(ia[i:i + 1] * N_KEYS + ib[0:n_half])
    cs = jnp.concatenate(cs, axis=0)
    row = lax.broadcasted_iota(jnp.int32, (n_cand, tm), 0).astype(F32)
    code = row * float(n_exp) + jnp.concatenate(ce, axis=0)
    slot = lax.broadcasted_iota(jnp.int32, (TOPK, tm), 0)
    best = jnp.zeros((TOPK, tm), F32)
    sel = jnp.zeros((TOPK, tm), F32)
    for i in range(TOPK):
        m = jnp.max(cs, axis=0, keepdims=True)
        first = jnp.min(jnp.where(cs == m, code, float(n_cand * n_exp)), axis=0, keepdims=True)
        best = jnp.where(slot == i, m, best)
        sel = jnp.where(slot == i, first, sel)
        cs = jnp.where(code == first, -jnp.inf, cs)
    sel = sel - float(n_exp) * jnp.floor(sel * (1.0 / n_exp))
    ex = jnp.exp(best - best[0:1])
    wts = ex / jnp.sum(ex, axis=0, keepdims=True)
    r0 = pl.multiple_of(hd * TOPK, TOPK)
    idx_t_scr[pl.ds(r0, TOPK), t0:t0 + tm] = sel
    w_t_scr[pl.ds(r0, TOPK), t0:t0 + tm] = wts
    return wts[TOPK - 1:TOPK] * 0.0


def _pack_table(tab):
    bits = lax.bitcast_convert_type(tab.astype(TABLE_DT), jnp.uint16).astype(jnp.uint32)
    word = bits[:, :HALF_D] | (bits[:, HALF_D:] << 16)
    return word.reshape(tab.shape[0] * SLAB, 128)


def _gather_rows(sidx_ref, slot, j, tab_ref, g_ref):
    for m in range(N_SEL):
        e = pl.multiple_of(sidx_ref[slot, j, m], SLAB)
        g_ref[pl.ds(m * SLAB, SLAB), :] = tab_ref[pl.ds(e, SLAB), :]
    chunks = [g_ref[pl.ds(c, N_SEL, stride=SLAB), :] for c in range(SLAB)]
    return pltpu.bitcast(jnp.concatenate(chunks, axis=1), TABLE_DT)


def _two_rows(r0, r1, width):
    sub = lax.broadcasted_iota(jnp.int32, (8, width), 0)
    return jnp.where(sub == 0, r0, jnp.where(sub == 1, r1, 0.0))


def _idx_copy(idx_hbm, sidx_ref, sem, row0, slot):
    return pltpu.make_async_copy(idx_hbm.at[pl.ds(row0, STAGE), :], sidx_ref.at[slot], sem.at[slot])


def _for_token_groups(idx_hbm, sidx_ref, sem, tm, process):
    i = pl.program_id(0)
    total = pl.num_programs(0) * tm
    tok0 = i * tm

    @pl.when(i == 0)
    def _():
        _idx_copy(idx_hbm, sidx_ref, sem, 0, 0).start()

    def stage(slot, base):
        for q in range(STAGE // GROUP):
            process(slot, q * GROUP, base + q * GROUP)

    def pair(k, carry):
        base = pl.multiple_of(k * (2 * STAGE), 2 * STAGE)
        _idx_copy(idx_hbm, sidx_ref, sem, 0, 0).wait()
        _idx_copy(idx_hbm, sidx_ref, sem, tok0 + base + STAGE, 1).start()
        stage(0, base)
        _idx_copy(idx_hbm, sidx_ref, sem, 0, 1).wait()
        nxt = tok0 + base + 2 * STAGE

        @pl.when(nxt < total)
        def _():
            _idx_copy(idx_hbm, sidx_ref, sem, nxt, 0).start()

        stage(1, base + STAGE)
        return carry

    lax.fori_loop(0, tm // (2 * STAGE), pair, 0)


def _peer_select_in_kernel(x_ref, sh_ref, sc_ref, g_ref, wq_ref, keys_ref, ex_ref, tab_ref, a_ref, idx_ref,
                           q_scr, idx_t_scr, w_t_scr, h2_scr, idx_scr, w_scr, g0_ref, g1_ref, act_ref,
                           sidx_ref, sem):
    i = pl.program_id(0)
    tm = x_ref.shape[0]
    n_it = tm // (2 * STAGE)
    hps = PEER_HEADS // (2 * n_it)
    assert hps * 2 * n_it == PEER_HEADS
    wr = i % 2
    rd = 1 - wr

    @pl.when(i == 0)
    def _():
        idx_scr[1] = jnp.zeros(idx_scr.shape[1:], jnp.int32)
        w_scr[1] = jnp.zeros(w_scr.shape[1:], F32)
        h2_scr[1] = jnp.zeros(h2_scr.shape[1:], F32)

    def stage_copy(row0, slot):
        return pltpu.make_async_copy(idx_scr.at[rd, pl.ds(row0, STAGE), :], sidx_ref.at[slot], sem.at[slot])

    stage_copy(0, 0).start()
    h2 = _norm_mod(x_ref[...], g_ref[...], sh_ref[0], sc_ref[0])
    h2_scr[wr] = h2
    q_scr[...] = _dot(h2.astype(BF16), wq_ref[...])

    lane = lax.broadcasted_iota(jnp.int32, (8, 2 * N_SEL), 1)
    even = (lane & 1) == 0
    sub = lax.broadcasted_iota(jnp.int32, (8, 2 * N_SEL), 0)

    def process(slot, j0, base):
        xg = h2_scr[rd, pl.ds(base, GROUP), :]
        rows = jnp.zeros((GROUP, 2 * N_SEL), F32)
        for j in range(GROUP):
            g = _gather_rows(sidx_ref, slot, j0 + j, tab_ref, g0_ref if j % 2 == 0 else g1_ref)
            lhs = _two_rows(xg[j:j + 1, :HALF_D], xg[j:j + 1, HALF_D:], HALF_D).astype(TABLE_DT)
            out = _dot_nt(lhs, g)
            z = jnp.where(even, out, pltpu.roll(out, 7, axis=0))
            s = z + pltpu.roll(z, 2 * N_SEL - 1, axis=1)
            rows = jnp.where(sub == j, s[0:1], rows)
        act_ref[pl.ds(base, GROUP), :] = rows

    def stage(slot, base):
        for q in range(STAGE // GROUP):
            process(slot, q * GROUP, base + q * GROUP)

    def retrieve(hd0):
        zero = None
        for k in range(hps):
            for t0 in range(0, tm, 128):
                zero = _topk_head(q_scr, keys_ref, idx_t_scr, w_t_scr, hd0 + k, t0, 128, after=zero)

    def body(it, carry):
        base = pl.multiple_of(it * (2 * STAGE), 2 * STAGE)
        stage_copy(0, 0).wait()
        stage_copy(base + STAGE, 1).start()
        retrieve(2 * hps * it)
        stage(0, base)
        stage_copy(0, 1).wait()

        @pl.when(it + 1 < n_it)
        def _():
            stage_copy(base + 2 * STAGE, 0).start()

        retrieve(2 * hps * it + hps)
        stage(1, base + STAGE)
        return carry

    lax.fori_loop(0, n_it, body, 0)
    ids = (idx_t_scr[...].T).astype(jnp.int32) * SLAB
    idx_scr[wr] = ids
    idx_ref[...] = ids
    w_scr[wr] = w_t_scr[...].T
    w2 = _dot_exact(w_scr[rd], ex_ref[...])
    a_ref[...] = jax.nn.gelu(act_ref[...]) * w2


def _peer_v_kernel(idx_hbm, a_ref, x_ref, gt_ref, tab_ref, o_ref, g0_ref, g1_ref, sidx_ref, sem):
    tm = a_ref.shape[0]
    sub = lax.broadcasted_iota(jnp.int32, (8, D_MODEL), 0)

    def process(slot, j0, base):
        ag = a_ref[pl.ds(base, GROUP), :]
        rows = jnp.zeros((GROUP, D_MODEL), F32)
        for j in range(GROUP):
            g = _gather_rows(sidx_ref, slot, j0 + j, tab_ref, g0_ref if j % 2 == 0 else g1_ref)
            a0 = jnp.broadcast_to(ag[j:j + 1, :], (8, 2 * N_SEL))
            lhs = _two_rows(a0, pltpu.roll(a0, 1, axis=1), 2 * N_SEL).astype(TABLE_DT)
            out = _dot(lhs, g)
            rows = jnp.where(sub == j, jnp.concatenate([out[0:1], out[1:2]], axis=1), rows)
        o_ref[pl.ds(base, GROUP), :] = x_ref[pl.ds(base, GROUP), :] + gt_ref[0] * rows

    _for_token_groups(idx_hbm, sidx_ref, sem, tm, process)


def _peer_ffn(x, shift, scale, gate, g, wq, keys, utab, vtab):
    B, T, D = x.shape
    N = B * T
    expand = np.zeros((N_SEL, 2 * N_SEL), np.float32)
    expand[np.arange(N_SEL), 2 * np.arange(N_SEL)] = 1.0
    flat = lambda i: (i, 0)
    gscr = pltpu.VMEM((N_SEL * SLAB, 128), jnp.uint32)
    stage = [pltpu.SMEM((2, STAGE, N_SEL), jnp.int32), pltpu.SemaphoreType.DMA((2,))]
    x2 = x.reshape(N, D)

    ts = 256
    nt = N // ts
    tpb_s = T // ts
    WQ = wq.shape[1]
    tile = lambda i: jnp.minimum(i, nt - 1)
    a, idx2 = pl.pallas_call(
        _peer_select_in_kernel,
        grid=(nt + 1,),
        in_specs=[pl.BlockSpec((ts, D), lambda i: (tile(i), 0)),
                  pl.BlockSpec((1, 1, D), lambda i: (tile(i) // tpb_s, 0, 0)),
                  pl.BlockSpec((1, 1, D), lambda i: (tile(i) // tpb_s, 0, 0)),
                  pl.BlockSpec((1, D), lambda i: (0, 0)),
                  pl.BlockSpec((D, WQ), lambda i: (0, 0)),
                  pl.BlockSpec((2, N_KEYS, N_KEYS), lambda i: (0, 0, 0)),
                  pl.BlockSpec((N_SEL, 2 * N_SEL), lambda i: (0, 0)),
                  pl.BlockSpec(memory_space=pltpu.VMEM)],
        out_specs=[pl.BlockSpec((ts, 2 * N_SEL), lambda i: (jnp.maximum(i - 1, 0), 0)),
                   pl.BlockSpec((ts, N_SEL), lambda i: (tile(i), 0))],
        out_shape=[jax.ShapeDtypeStruct((N, 2 * N_SEL), F32), jax.ShapeDtypeStruct((N, N_SEL), jnp.int32)],
        scratch_shapes=[pltpu.VMEM((ts, WQ), F32), pltpu.VMEM((N_SEL, ts), F32), pltpu.VMEM((N_SEL, ts), F32),
                        pltpu.VMEM((2, ts, D), F32), pltpu.VMEM((2, ts, N_SEL), jnp.int32),
                        pltpu.VMEM((2, ts, N_SEL), F32), gscr, gscr, pltpu.VMEM((ts, 2 * N_SEL), F32)] + stage,
        compiler_params=_params("arbitrary"),
        name="peer_select_in",
    )(x2, shift, scale, g, wq, keys, jnp.asarray(expand), utab)

    tm = 128
    tpb = T // tm
    out = pl.pallas_call(
        _peer_v_kernel,
        grid=(N // tm,),
        in_specs=[pl.BlockSpec(memory_space=pl.ANY),
                  pl.BlockSpec((tm, 2 * N_SEL), flat),
                  pl.BlockSpec((tm, D), flat),
                  pl.BlockSpec((1, 1, D), lambda i: (i // tpb, 0, 0)),
                  pl.BlockSpec(memory_space=pltpu.VMEM)],
        out_specs=pl.BlockSpec((tm, D), flat),
        out_shape=jax.ShapeDtypeStruct((N, D), F32),
        scratch_shapes=[gscr, gscr] + stage,
        compiler_params=_params("arbitrary"),
        name="peer_expert_out",
    )(idx2, a, x2, gate, vtab)
    return out.reshape(B, T, D)


def _halo_rows(qk, C):
    B, T, _ = qk.shape
    z = jnp.zeros((B, 1, C), qk.dtype)
    prev = jnp.concatenate([z, qk[:, CHUNK - 1:T - 1:CHUNK, :C]], axis=1)
    nxt = jnp.concatenate([qk[:, CHUNK::CHUNK, :C], z], axis=1)
    return jnp.stack([prev, nxt], axis=2)


def _stream_layer(x, mod, lw, init):
    B, T, D = x.shape
    sh1, sc1, gt1, sh2, sc2, gt2 = mod
    qkvo, gd, gd_t = _mlstm_proj(x, sh1, sc1, lw["g1"], lw["w_qkvo"], lw["b_qkvo"], lw["w_gate"], lw["b_gate"])
    sgu, fa, fb = _sgu_fnet_proj(x, sh1, sc1, lw["g1"], lw["w_sf"], lw["b_sf"], lw["sgu_w"], lw["sgu_bt"],
                                 lw["dft_c"], lw["dft_s"])
    halo = _halo_rows(qkvo, 2 * MIX_W)
    hdir, c_fin, n_fin, m_fin = _mlstm_scan(qkvo, halo, gd, gd_t, lw["conv"], *init)
    fnet = _fnet_seq_dft_long(fa, fb) if T > 2 * CHUNK else _fnet_seq_dft_short(fa, fb)
    x = _merge(x, sh1, sc1, gt1, lw["g1"], hdir, qkvo, lw["mnorm_g"], sgu, fnet,
               lw["w_mg"], lw["b_mg"], lw["w_br"], lw["w_out"])
    x = _peer_ffn(x, sh2, sc2, gt2, lw["g2"], lw["wq"], lw["keys"], lw["utab"], lw["vtab"])
    return x, (c_fin, n_fin, m_fin)


def _pos_embed(rows):
    quarter = D_MODEL // 4
    omega = 1.0 / (10000.0 ** (jnp.arange(quarter, dtype=F32) / quarter))
    r = jnp.repeat(jnp.arange(rows, dtype=F32), GRID_W)[:, None] * omega
    cc = jnp.tile(jnp.arange(GRID_W, dtype=F32), rows)[:, None] * omega
    return jnp.concatenate([jnp.sin(r), jnp.cos(r), jnp.sin(cc), jnp.cos(cc)], axis=-1)


def kernel(x, c, ctx, c_ctx, w_mod, b_mod, norm1_g, norm2_g, w_in, b_in, conv_qk, mlstm_norm_g, sgu_w, sgu_b, w_br, w_out, peer_wq, peer_keys, peer_u, peer_v, final_g):
    B, T, D = x.shape
    depth = w_mod.shape[0]
    x = _add_pos(x, _pos_embed(T // GRID_W))

    cc = jnp.zeros((8, D), F32).at[:B].set(c).at[B].set(c_ctx)
    mods = _modulation(cc, w_mod, b_mod)

    cdft, sdft = _dft_mats(GROUP_W)
    dft_c = jnp.asarray(cdft, BF16)
    dft_s = jnp.asarray(sdft, BF16)
    o_q, o_g = 4 * MIX_W, 4 * MIX_W + 4 * N_HEADS
    o_sf, o_mg = o_g, o_g + 3 * MIX_W

    def layer(carry, lp):
        x, ctx = carry
        (mod, g1, g2, win, bin_, conv, mng, sw, sb, wbr, wout, wq, keys, pu, pv) = lp
        lw = {
            "g1": g1.reshape(1, D), "g2": g2.reshape(1, D),
            "w_qkvo": win[:, :o_q].astype(BF16), "b_qkvo": bin_[:o_q].reshape(1, -1),
            "w_gate": jnp.pad(win[:, o_q:o_g].reshape(D, 2, 2 * N_HEADS),
                              ((0, 0), (0, 0), (0, 128 - 2 * N_HEADS))).reshape(D, 256),
            "b_gate": jnp.pad(bin_[o_q:o_g].reshape(2, 2 * N_HEADS), ((0, 0), (0, 128 - 2 * N_HEADS))).reshape(1, 256),
            "w_sf": win[:, o_sf:o_mg].astype(BF16), "b_sf": bin_[o_sf:o_mg].reshape(1, -1),
            "w_mg": win[:, o_mg:].astype(BF16), "b_mg": bin_[o_mg:].reshape(1, -1),
            "conv": conv, "mnorm_g": mng.reshape(1, -1),
            "sgu_w": sw.astype(BF16), "sgu_bt": sb.T,
            "dft_c": dft_c, "dft_s": dft_s,
            "w_br": wbr.astype(BF16), "w_out": wout.astype(BF16),
            "wq": wq.astype(BF16), "keys": keys.astype(BF16),
            "utab": _pack_table(pu), "vtab": _pack_table(pv),
        }
        mod_x = [mod[:B, i * D:(i + 1) * D].reshape(B, 1, D) for i in range(N_MOD)]
        mod_c = [jnp.broadcast_to(mod[B, i * D:(i + 1) * D].reshape(1, 1, D), (B, 1, D)) for i in range(N_MOD)]
        zero = (jnp.zeros((2, B, N_HEADS, HEAD_D, HEAD_D), F32),
                jnp.zeros((2, B, N_HEADS, 1, HEAD_D), F32),
                jnp.zeros((2, B, N_HEADS, 1, HEAD_D), F32))
        ctx_new, ctx_state = _stream_layer(ctx, mod_c, lw, zero)
        x_new, _ = _stream_layer(x, mod_x, lw, ctx_state)
        return (x_new, ctx_new), None

    xs = (mods, norm1_g, norm2_g, w_in, b_in, conv_qk, mlstm_norm_g, sgu_w, sgu_b, w_br, w_out,
          peer_wq, peer_keys, peer_u, peer_v)
    (x, ctx), _ = lax.scan(layer, (x, ctx), xs)
    return _final_norm(x, final_g)
```
